```python
import jax, jax.numpy as jnp
from jax import lax
import numpy as np

D_MODEL = 2048
BATCH = 2
SEQ = 4096
DEPTH = 4
DEC_BATCH = 8
DEC_SEQ = 1
PAST_LEN = 16384
PAGE_SIZE = 128

N_MIXERS = 4
FFN_DIM = 5632
NORM_EPS = 1e-6
NEG_INF = -1e30
HEAD_DIM = 128
A_HEADS = D_MODEL // HEAD_DIM
MOBA_BLOCK = 256
MOBA_TOPK = 3
MOBA_Q_CHUNK = 16
POOL_WINDOWS = (2, 4, 8, 16)
POOL_GROUP = D_MODEL // len(POOL_WINDOWS)
POOL_STATE = max(POOL_WINDOWS) - 1
C_HEADS = D_MODEL // HEAD_DIM
C_KV_HEADS = 4
C_GROUP = C_HEADS // C_KV_HEADS
IDX_HEADS = 16
IDX_DIM = 64
DSA_TOPK = 256
DSA_Q_BLOCK = 128
GLA_HEADS = 4
GLA_DK = D_MODEL // 2 // GLA_HEADS
GLA_DV = D_MODEL // GLA_HEADS
GLA_GATE_RANK = 16
GLA_TAU = 16.0
GLA_CHUNK = 64

kernel_name = 'hybrid_moba_pool_dsa_gla_step'

F32 = jnp.float32


def rms_norm(x, g):
    xf = x.astype(F32)
    y = xf * lax.rsqrt(jnp.mean(xf * xf, axis=-1, keepdims=True) + NORM_EPS)
    return (y * g.astype(F32)).astype(x.dtype)


def swiglu(x, w_gate, w_up, w_down):
    return (jax.nn.silu(x @ w_gate) * (x @ w_up)) @ w_down


def split_cols(y, sizes):
    return jnp.split(y, [int(s) for s in np.cumsum(sizes)[:-1]], axis=-1)


def gather_rows(pool, page_table, pos):
    bidx = jnp.arange(pos.shape[0]).reshape((-1,) + (1,) * (pos.ndim - 1))
    return pool[page_table[bidx, pos // PAGE_SIZE], pos % PAGE_SIZE]


def moba_qkv(h, w_qkv, g_q, g_k):
    b, l, _ = h.shape
    q, k, v = jnp.split(h @ w_qkv, 3, axis=-1)
    shp = (b, l, A_HEADS, HEAD_DIM)
    return rms_norm(q.reshape(shp), g_q), rms_norm(k.reshape(shp), g_k), v.reshape(shp)


def moba_prompt(h, w_qkv, w_o, g_q, g_k):
    b, l, _ = h.shape
    q, k, v = moba_qkv(h, w_qkv, g_q, g_k)
    nb = -(-l // MOBA_BLOCK)
    pad = nb * MOBA_BLOCK - l

    def blocks(t):
        t = jnp.pad(t, ((0, 0), (0, pad), (0, 0), (0, 0)))
        return t.reshape(b, nb, MOBA_BLOCK, A_HEADS, HEAD_DIM).transpose(0, 3, 1, 2, 4)

    kb, vb = blocks(k), blocks(v)
    k_mean = jnp.mean(kb.astype(F32), axis=3)
    n_sel = min(MOBA_TOPK, (l - 1) // MOBA_BLOCK)
    qh = q.transpose(0, 2, 1, 3)
    bi = jnp.arange(b)[:, None, None, None]
    hi = jnp.arange(A_HEADS)[None, :, None, None]
    scale = HEAD_DIM ** -0.5

    def chunk(c):
        t0 = c * MOBA_Q_CHUNK
        qc = lax.dynamic_slice_in_dim(qh, t0, MOBA_Q_CHUNK, axis=2)
        tq = t0 + jnp.arange(MOBA_Q_CHUNK)
        j = t0 // MOBA_BLOCK
        k_own = lax.dynamic_index_in_dim(kb, j, axis=2, keepdims=False)
        v_own = lax.dynamic_index_in_dim(vb, j, axis=2, keepdims=False)
        kpos = j * MOBA_BLOCK + jnp.arange(MOBA_BLOCK)
        s_own = jnp.einsum('bhqd,bhkd->bhqk', qc, k_own).astype(F32) * scale
        s_own = jnp.where(kpos[None, :] <= tq[:, None], s_own, NEG_INF)
        if n_sel == 0:
            p = jax.nn.softmax(s_own, axis=-1).astype(v.dtype)
            return jnp.einsum('bhqk,bhkd->bhqd', p, v_own)
        gate = jnp.einsum('bhqd,bhnd->bhqn', qc.astype(F32), k_mean)
        gate = jnp.where(jnp.arange(nb) < j, gate, NEG_INF)
        _, sel = lax.top_k(gate, n_sel)
        k_sel = kb[bi, hi, sel]
        s_sel = jnp.einsum('bhqd,bhqnkd->bhqnk', qc, k_sel).astype(F32) * scale
        s_sel = jnp.where((sel < j)[..., None], s_sel, NEG_INF)
        n_s = n_sel * MOBA_BLOCK
        s = jnp.concatenate([s_sel.reshape(b, A_HEADS, MOBA_Q_CHUNK, n_s), s_own], axis=-1)
        p = jax.nn.softmax(s, axis=-1).astype(v.dtype)
        p_sel = p[..., :n_s].reshape(b, A_HEADS, MOBA_Q_CHUNK, n_sel, MOBA_BLOCK)
        return (jnp.einsum('bhqnk,bhqnkd->bhqd', p_sel, vb[bi, hi, sel])
                + jnp.einsum('bhqk,bhkd->bhqd', p[..., n_s:], v_own))

    o = lax.map(chunk, jnp.arange(l // MOBA_Q_CHUNK))
    o = o.transpose(1, 0, 3, 2, 4).reshape(b, l, A_HEADS * HEAD_DIM)
    return (o @ w_o).astype(h.dtype), k, v


def moba_sample(h, pool_k, pool_v, page_table, w_qkv, w_o, g_q, g_k):
    b, T, _ = h.shape
    q, k, v = moba_qkv(h, w_qkv, g_q, g_k)
    n_full = PAST_LEN // MOBA_BLOCK
    own0 = n_full * MOBA_BLOCK
    n_sel = min(MOBA_TOPK, n_full)
    scale = HEAD_DIM ** -0.5
    qpos = PAST_LEN + jnp.arange(T)
    pos_r = jnp.broadcast_to(jnp.arange(own0, PAST_LEN), (b, PAST_LEN - own0))
    k_own = jnp.concatenate([gather_rows(pool_k, page_table, pos_r), k], axis=1)
    v_own = jnp.concatenate([gather_rows(pool_v, page_table, pos_r), v], axis=1)
    kpos = jnp.arange(own0, PAST_LEN + T)
    s_own = jnp.einsum('bthd,bshd->bths', q, k_own).astype(F32) * scale
    s_own = jnp.where((kpos[None, :] <= qpos[:, None])[None, :, None, :], s_own, NEG_INF)
    if n_sel == 0:
        p = jax.nn.softmax(s_own, axis=-1).astype(v.dtype)
        o = jnp.einsum('bths,bshd->bthd', p, v_own)
    else:
        pos_f = jnp.broadcast_to(jnp.arange(own0), (b, own0))
        k_full = gather_rows(pool_k, page_table, pos_f).astype(F32)
        k_mean = jnp.mean(k_full.reshape(b, n_full, MOBA_BLOCK, A_HEADS, HEAD_DIM), axis=2)
        gate = jnp.einsum('bthd,bnhd->bthn', q.astype(F32), k_mean)
        _, sel = lax.top_k(gate, n_sel)
        pos = sel[..., None] * MOBA_BLOCK + jnp.arange(MOBA_BLOCK)
        bi = jnp.arange(b)[:, None, None, None, None]
        hi = jnp.arange(A_HEADS)[None, None, :, None, None]
        phys = page_table[bi, pos // PAGE_SIZE]
        off = pos % PAGE_SIZE
        k_sel = pool_k[phys, off, hi]
        v_sel = pool_v[phys, off, hi]
        n_s = n_sel * MOBA_BLOCK
        s_sel = jnp.einsum('bthd,bthnkd->bthnk', q, k_sel).astype(F32) * scale
        s = jnp.concatenate([s_sel.reshape(b, T, A_HEADS, n_s), s_own], axis=-1)
        p = jax.nn.softmax(s, axis=-1).astype(v.dtype)
        p_sel = p[..., :n_s].reshape(b, T, A_HEADS, n_sel, MOBA_BLOCK)
        o = (jnp.einsum('bthnk,bthnkd->bthd', p_sel, v_sel)
             + jnp.einsum('bths,bshd->bthd', p[..., n_s:], v_own))
    o = o.reshape(b, T, A_HEADS * HEAD_DIM)
    return (o @ w_o).astype(h.dtype), k, v


def pool_mix(h, buf, w_pool, scale):
    b, l, _ = h.shape
    p = buf.shape[1]
    xs = jnp.concatenate([buf.astype(h.dtype), h], axis=1).astype(F32)
    csum = jnp.concatenate([jnp.zeros_like(xs[:, :1]), jnp.cumsum(xs, axis=1)], axis=1)
    hi = p + jnp.arange(l) + 1
    parts = []
    for g, w in enumerate(POOL_WINDOWS):
        sl = slice(g * POOL_GROUP, (g + 1) * POOL_GROUP)
        lo = jnp.maximum(hi - w, 0)
        cg = csum[..., sl]
        mean = (cg[:, hi] - cg[:, lo]) / (hi - lo).astype(F32)[None, :, None]
        parts.append(mean - xs[:, p:, sl])
    pooled = jnp.stack(parts, axis=2)
    y = jnp.einsum('blgc,gcd->blgd', pooled, w_pool.astype(F32)).reshape(b, l, D_MODEL)
    y = y * scale.astype(F32)
    return y.astype(h.dtype), xs[:, -POOL_STATE:].astype(h.dtype)


def dsa_proj(h, w_in, g_q, g_k):
    b, l, _ = h.shape
    q, k, v, qi, ki, wi = split_cols(h @ w_in, (C_HEADS * HEAD_DIM, C_KV_HEADS * HEAD_DIM, C_KV_HEADS * HEAD_DIM,
                                                IDX_HEADS * IDX_DIM, IDX_DIM, IDX_HEADS))
    q = rms_norm(q.reshape(b, l, C_HEADS, HEAD_DIM), g_q)
    k = rms_norm(k.reshape(b, l, C_KV_HEADS, HEAD_DIM), g_k)
    v = v.reshape(b, l, C_KV_HEADS, HEAD_DIM)
    qi = qi.reshape(b, l, IDX_HEADS, IDX_DIM)
    return q, k, v, qi, ki, wi


def index_scores(qi, wi, ki):
    dots = jnp.einsum('bthe,bse->bths', qi.astype(F32), ki.astype(F32)) * IDX_DIM ** -0.5
    return jnp.einsum('bth,bths->bts', wi.astype(F32) * IDX_HEADS ** -0.5, jax.nn.relu(dots))


def gathered_attention(q, k_sel, v_sel, valid):
    b, t = q.shape[:2]
    qg = q.reshape(b, t, C_KV_HEADS, C_GROUP, HEAD_DIM)
    s = jnp.einsum('btgjd,btkgd->btgjk', qg, k_sel).astype(F32) * HEAD_DIM ** -0.5
    s = jnp.where(valid[:, :, None, None, :], s, NEG_INF)
    p = jax.nn.softmax(s, axis=-1).astype(v_sel.dtype)
    return jnp.einsum('btgjk,btkgd->btgjd', p, v_sel).reshape(b, t, C_HEADS * HEAD_DIM)


def dsa_prompt(h, w_in, w_o, g_q, g_k):
    b, l, _ = h.shape
    q, k, v, qi, ki, wi = dsa_proj(h, w_in, g_q, g_k)
    n_top = min(DSA_TOPK, l // 4)
    kpos = jnp.arange(l)
    bi = jnp.arange(b)[:, None, None]

    def block(c):
        t0 = c * DSA_Q_BLOCK
        sl = lambda t: lax.dynamic_slice_in_dim(t, t0, DSA_Q_BLOCK, axis=1)
        tq = t0 + jnp.arange(DSA_Q_BLOCK)
        score = index_scores(sl(qi), sl(wi), ki)
        score = jnp.where(kpos[None, None, :] <= tq[None, :, None], score, NEG_INF)
        _, idx = lax.top_k(score, n_top)
        valid = idx <= tq[None, :, None]
        return gathered_attention(sl(q), k[bi, idx], v[bi, idx], valid)

    o = lax.map(block, jnp.arange(l // DSA_Q_BLOCK))
    o = o.transpose(1, 0, 2, 3).reshape(b, l, C_HEADS * HEAD_DIM)
    return (o @ w_o).astype(h.dtype), k, v, ki


def dsa_sample(h, pool_k, pool_v, pool_ki, page_table, w_in, w_o, g_q, g_k):
    b, T, _ = h.shape
    q, k, v, qi, ki, wi = dsa_proj(h, w_in, g_q, g_k)
    l = PAST_LEN + T
    n_top = min(DSA_TOPK, l // 4)
    past_pos = jnp.broadcast_to(jnp.arange(PAST_LEN), (b, PAST_LEN))
    ki_all = jnp.concatenate([gather_rows(pool_ki, page_table, past_pos), ki], axis=1)
    tq = PAST_LEN + jnp.arange(T)
    score = index_scores(qi, wi, ki_all)
    score = jnp.where(jnp.arange(l)[None, None, :] <= tq[None, :, None], score, NEG_INF)
    _, idx = lax.top_k(score, n_top)
    valid = idx <= tq[None, :, None]
    in_past = (idx < PAST_LEN)[..., None, None]
    bi = jnp.arange(b)[:, None, None]
    new_i = jnp.clip(idx - PAST_LEN, 0, T - 1)
    past_i = jnp.minimum(idx, PAST_LEN - 1)
    k_sel = jnp.where(in_past, gather_rows(pool_k, page_table, past_i), k[bi, new_i])
    v_sel = jnp.where(in_past, gather_rows(pool_v, page_table, past_i), v[bi, new_i])
    o = gathered_attention(q, k_sel, v_sel, valid)
    return (o @ w_o).astype(h.dtype), k, v, ki


def gla_mix(h, s0, w_in, w_g2, b_g, g_n, w_o):
    b, l, _ = h.shape
    dk, dv = GLA_HEADS * GLA_DK, GLA_HEADS * GLA_DV
    q, k, v, r, low = split_cols(h @ w_in, (dk, dk, dv, dv, GLA_GATE_RANK))
    g = jax.nn.log_sigmoid((low @ w_g2 + b_g).astype(F32)) / GLA_TAU
    c = min(GLA_CHUNK, l)
    nc = l // c

    def heads(t, d):
        return t.astype(F32).reshape(b, nc, c, GLA_HEADS, d).transpose(1, 0, 3, 2, 4)

    xs = (heads(q, GLA_DK) * GLA_DK ** -0.5, heads(k, GLA_DK), heads(v, GLA_DV), heads(g, GLA_DK))
    causal = jnp.tril(jnp.ones((c, c), dtype=bool))

    def step(state, inp):
        qc, kc, vc, gc = inp
        bcum = jnp.cumsum(gc, axis=2)
        diff = bcum[:, :, :, None, :] - bcum[:, :, None, :, :]
        decay = jnp.exp(jnp.where(causal[:, :, None], diff, -jnp.inf))
        att = jnp.sum(qc[:, :, :, None, :] * kc[:, :, None, :, :] * decay, axis=-1)
        o = (jnp.einsum('bhts,bhsv->bhtv', att, vc)
             + jnp.einsum('bhtd,bhdv->bhtv', qc * jnp.exp(bcum), state))
        b_last = bcum[:, :, -1:, :]
        state = (jnp.exp(b_last[:, :, 0, :, None]) * state
                 + jnp.einsum('bhsd,bhsv->bhdv', kc * jnp.exp(b_last - bcum), vc))
        return state, o

    s_fin, o = lax.scan(step, s0.astype(F32), xs)
    o = o.transpose(1, 0, 3, 2, 4).reshape(b, l, GLA_HEADS, GLA_DV).astype(h.dtype)
    o = rms_norm(o, g_n).reshape(b, l, dv) * jax.nn.silu(r)
    return (o @ w_o).astype(h.dtype), s_fin


def setup_inputs(seed: int = 0) -> dict:
    key = jax.random.key(seed)
    keys = iter(jax.random.split(key, 48))

    def normal(shape, scale=1.0):
        return jax.random.normal(next(keys), shape, F32) * scale

    def gain(shape):
        return 1.0 + 0.02 * normal(shape)

    n_pages = PAST_LEN // PAGE_SIZE
    n_pool = (5 * DEC_BATCH * n_pages + 3) // 4
    x_prompt = normal((BATCH, SEQ, D_MODEL))
    x_sample = normal((DEC_BATCH, DEC_SEQ, D_MODEL))
    cache_a_k = normal((n_pool, PAGE_SIZE, A_HEADS, HEAD_DIM))
    cache_a_v = normal((n_pool, PAGE_SIZE, A_HEADS, HEAD_DIM))
    state_b_pool = normal((DEC_BATCH, POOL_STATE, D_MODEL))
    cache_c_k = normal((n_pool, PAGE_SIZE, C_KV_HEADS, HEAD_DIM))
    cache_c_v = normal((n_pool, PAGE_SIZE, C_KV_HEADS, HEAD_DIM))
    cache_c_idx_k = normal((n_pool, PAGE_SIZE, IDX_DIM))
    state_d_gla = normal((DEC_BATCH, GLA_HEADS, GLA_DK, GLA_DV))
    perm = jax.random.permutation(next(keys), n_pool)
    page_table = perm[: DEC_BATCH * n_pages].reshape(DEC_BATCH, n_pages).astype(jnp.int32)
    c_in = C_HEADS * HEAD_DIM + 2 * C_KV_HEADS * HEAD_DIM + IDX_HEADS * IDX_DIM + IDX_DIM + IDX_HEADS
    d_in = 2 * GLA_HEADS * GLA_DK + 2 * GLA_HEADS * GLA_DV + GLA_GATE_RANK
    sd, sf = D_MODEL ** -0.5, FFN_DIM ** -0.5
    return {
        'x_prompt': x_prompt, 'x_sample': x_sample,
        'cache_a_k': cache_a_k, 'cache_a_v': cache_a_v, 'state_b_pool': state_b_pool,
        'cache_c_k': cache_c_k, 'cache_c_v': cache_c_v, 'cache_c_idx_k': cache_c_idx_k,
        'state_d_gla': state_d_gla, 'page_table': page_table,
        'norm_ffn1': gain((DEPTH, D_MODEL)),
        'ffn1_w_gate': normal((DEPTH, D_MODEL, FFN_DIM), sd),
        'ffn1_w_up': normal((DEPTH, D_MODEL, FFN_DIM), sd),
        'ffn1_w_down': normal((DEPTH, FFN_DIM, D_MODEL), sf),
        'norm_mix': gain((DEPTH, D_MODEL)),
        'norm_ffn2': gain((DEPTH, D_MODEL)),
        'ffn2_w_gate': normal((DEPTH, D_MODEL, FFN_DIM), sd),
        'ffn2_w_up': normal((DEPTH, D_MODEL, FFN_DIM), sd),
        'ffn2_w_down': normal((DEPTH, FFN_DIM, D_MODEL), sf),
        'a_w_qkv': normal((D_MODEL, 3 * A_HEADS * HEAD_DIM), sd),
        'a_w_o': normal((A_HEADS * HEAD_DIM, D_MODEL), (A_HEADS * HEAD_DIM) ** -0.5),
        'a_g_q': gain((HEAD_DIM,)), 'a_g_k': gain((HEAD_DIM,)),
        'b_w_pool': normal((len(POOL_WINDOWS), POOL_GROUP, POOL_GROUP), POOL_GROUP ** -0.5),
        'b_scale': gain((D_MODEL,)),
        'c_w_in': normal((D_MODEL, c_in), sd),
        'c_w_o': normal((C_HEADS * HEAD_DIM, D_MODEL), (C_HEADS * HEAD_DIM) ** -0.5),
        'c_g_q': gain((HEAD_DIM,)), 'c_g_k': gain((HEAD_DIM,)),
        'd_w_in': normal((D_MODEL, d_in), sd),
        'd_w_g2': normal((GLA_GATE_RANK, GLA_HEADS * GLA_DK), GLA_GATE_RANK ** -0.5),
        'd_b_g': normal((GLA_HEADS * GLA_DK,), 0.1),
        'd_g_n': gain((GLA_DV,)),
        'd_w_o': normal((GLA_HEADS * GLA_DV, D_MODEL), (GLA_HEADS * GLA_DV) ** -0.5),
    }


def reference(x_prompt, x_sample, cache_a_k, cache_a_v, state_b_pool, cache_c_k, cache_c_v, cache_c_idx_k,
              state_d_gla, page_table,
              norm_ffn1, ffn1_w_gate, ffn1_w_up, ffn1_w_down, norm_mix, norm_ffn2, ffn2_w_gate, ffn2_w_up,
              ffn2_w_down, a_w_qkv, a_w_o, a_g_q, a_g_k, b_w_pool, b_scale, c_w_in, c_w_o, c_g_q, c_g_k,
              d_w_in, d_w_g2, d_b_g, d_g_n, d_w_o):
    xp, xs = x_prompt, x_sample
    for i in range(DEPTH):
        xp = xp + 0.5 * swiglu(rms_norm(xp, norm_ffn1[i]), ffn1_w_gate[i], ffn1_w_up[i], ffn1_w_down[i])
        xs = xs + 0.5 * swiglu(rms_norm(xs, norm_ffn1[i]), ffn1_w_gate[i], ffn1_w_up[i], ffn1_w_down[i])
        hp, hs = rms_norm(xp, norm_mix[i]), rms_norm(xs, norm_mix[i])
        m = i % N_MIXERS
        if m == 0:
            op, a_k_p, a_v_p = moba_prompt(hp, a_w_qkv, a_w_o, a_g_q, a_g_k)
            osm, a_k_s, a_v_s = moba_sample(hs, cache_a_k, cache_a_v, page_table, a_w_qkv, a_w_o, a_g_q, a_g_k)
        elif m == 1:
            op, b_pool_p = pool_mix(hp, jnp.zeros((hp.shape[0], 0, D_MODEL), hp.dtype), b_w_pool, b_scale)
            osm, b_pool_s = pool_mix(hs, state_b_pool, b_w_pool, b_scale)
        elif m == 2:
            op, c_k_p, c_v_p, c_ik_p = dsa_prompt(hp, c_w_in, c_w_o, c_g_q, c_g_k)
            osm, c_k_s, c_v_s, c_ik_s = dsa_sample(hs, cache_c_k, cache_c_v, cache_c_idx_k, page_table,
                                                  c_w_in, c_w_o, c_g_q, c_g_k)
        else:
            s_zero = jnp.zeros((hp.shape[0], GLA_HEADS, GLA_DK, GLA_DV), F32)
            op, d_s_p = gla_mix(hp, s_zero, d_w_in, d_w_g2, d_b_g, d_g_n, d_w_o)
            osm, d_s_s = gla_mix(hs, state_d_gla, d_w_in, d_w_g2, d_b_g, d_g_n, d_w_o)
        xp = xp + op
        xs = xs + osm
        xp = xp + 0.5 * swiglu(rms_norm(xp, norm_ffn2[i]), ffn2_w_gate[i], ffn2_w_up[i], ffn2_w_down[i])
        xs = xs + 0.5 * swiglu(rms_norm(xs, norm_ffn2[i]), ffn2_w_gate[i], ffn2_w_up[i], ffn2_w_down[i])
    return (xp, xs, a_k_p, a_v_p, a_k_s, a_v_s, b_pool_p, b_pool_s, c_k_p, c_v_p, c_ik_p, c_k_s, c_v_s, c_ik_s,
            d_s_p, d_s_s)
```

```python
import functools

import jax
import jax.numpy as jnp
from jax import lax
from jax.experimental import pallas as pl
from jax.experimental.pallas import tpu as pltpu

F32 = jnp.float32
BF16 = jnp.bfloat16
I32 = jnp.int32

NORM_EPS = 1e-6
NEG_INF = -1e30
HEAD_DIM = 128
LANES = 128
PAGE_SIZE = 128
MOBA_BLOCK = 256
MOBA_TOPK = 3
POOL_WINDOWS = (2, 4, 8, 16)
POOL_HALO = 16
C_KV_HEADS = 4
IDX_HEADS = 16
IDX_DIM = 64
DSA_TOPK = 256
DSA_Q_TILE = 128
GLA_HEADS = 4
GLA_GATE_RANK = 16
GLA_TAU = 16.0
GLA_CHUNK = 64
GLA_SAMPLE_ROWS = 16
VMEM_LIMIT = 56 * 1024 * 1024

INT_MIN = -2147483648


def _cp(*sem):
    return pltpu.CompilerParams(dimension_semantics=sem, vmem_limit_bytes=VMEM_LIMIT)


def _dot(a, b):
    return jnp.dot(a, b, preferred_element_type=F32)


def _dot_nt(a, b):
    return lax.dot_general(a, b, (((1,), (1,)), ((), ())), preferred_element_type=F32)


def _dot_tn(a, b):
    return lax.dot_general(a, b, (((0,), (0,)), ((), ())), preferred_element_type=F32)


def _split3(a):
    a1 = a.astype(BF16)
    r1 = a - a1.astype(F32)
    a2 = r1.astype(BF16)
    a3 = (r1 - a2.astype(F32)).astype(BF16)
    return a1, a2, a3


def _dot_hi(a, b, dot=_dot):
    a1, a2, _ = _split3(a)
    b1, b2, _ = _split3(b)
    return dot(a1, b1) + (dot(a1, b2) + dot(a2, b1))


def _dot_exact_lhs(a01, b, dot=_dot):
    b1, b2, b3 = _split3(b)
    return dot(a01, b1) + (dot(a01, b2) + dot(a01, b3))


def _dot_exact_rhs(a, b01, dot=_dot):
    a1, a2, a3 = _split3(a)
    return dot(a1, b01) + (dot(a2, b01) + dot(a3, b01))


def _rms(x):
    return x * lax.rsqrt(jnp.mean(x * x, axis=-1, keepdims=True) + NORM_EPS)


def _silu(x):
    return x * jax.nn.sigmoid(x)


def _ffn_body(x_ref, g_ref, wg_ref, wu_ref, wd_ref, o_ref, xn_ref):
    j = pl.program_id(1)

    @pl.when(j == 0)
    def _():
        xn_ref[...] = (_rms(x_ref[...]) * g_ref[...]).astype(BF16)

    xn = xn_ref[...]
    a = _dot(xn, wg_ref[...].astype(BF16))
    u = _dot(xn, wu_ref[...].astype(BF16))
    part = _dot((_silu(a) * u).astype(BF16), wd_ref[...].astype(BF16))

    @pl.when(j == 0)
    def _():
        o_ref[...] = part

    @pl.when(j > 0)
    def _():
        o_ref[...] += part

    @pl.when(j == pl.num_programs(1) - 1)
    def _():
        o_ref[...] = x_ref[...] + 0.5 * o_ref[...]


def _ffn(x, g, wg, wu, wd, layer, *, tm, tf):
    m, d = x.shape
    f = wg.shape[-1]
    return pl.pallas_call(
        _ffn_body,
        grid=(m // tm, f // tf),
        in_specs=[
            pl.BlockSpec((tm, d), lambda i, j: (i, 0)),
            pl.BlockSpec((None, 1, d), lambda i, j: (layer, 0, 0)),
            pl.BlockSpec((None, d, tf), lambda i, j: (layer, 0, j)),
            pl.BlockSpec((None, d, tf), lambda i, j: (layer, 0, j)),
            pl.BlockSpec((None, tf, d), lambda i, j: (layer, j, 0)),
        ],
        out_specs=pl.BlockSpec((tm, d), lambda i, j: (i, 0)),
        out_shape=jax.ShapeDtypeStruct((m, d), F32),
        scratch_shapes=[pltpu.VMEM((tm, d), BF16)],
        compiler_params=_cp("parallel", "arbitrary"),
        name="ffn",
    )(x, g[:, None, :], wg, wu, wd)


def _proj_body(*refs, pre_norm, n_head_norm_tiles, residual, tn):
    it = iter(refs)
    x_ref = next(it)
    g_ref = next(it) if pre_norm else None
    w_ref = next(it)
    cg_ref = next(it) if n_head_norm_tiles else None
    r_ref = next(it) if residual else None
    o_ref = next(it)
    xn_ref = next(it)
    j = pl.program_id(1)

    @pl.when(j == 0)
    def _():
        x = x_ref[...]
        if pre_norm:
            x = _rms(x) * g_ref[...]
        xn_ref[...] = x.astype(BF16)

    y = _dot(xn_ref[...], w_ref[...].astype(BF16))
    if residual:
        y = r_ref[...] + y

    if n_head_norm_tiles:
        @pl.when(j < n_head_norm_tiles)
        def _():
            for c in range(tn // HEAD_DIM):
                sl = slice(c * HEAD_DIM, (c + 1) * HEAD_DIM)
                o_ref[:, sl] = _rms(y[:, sl]) * cg_ref[:, sl]

        @pl.when(j >= n_head_norm_tiles)
        def _():
            o_ref[...] = y
    else:
        o_ref[...] = y


def _proj(x, w, *, tm, tn, norm_gain=None, norm_layer=0, head_gain=None, n_head_norm_tiles=0,
          residual=None):
    m, k = x.shape
    n = w.shape[1]
    pre_norm = norm_gain is not None
    args = [x]
    specs = [pl.BlockSpec((tm, k), lambda i, j: (i, 0))]
    if pre_norm:
        args.append(norm_gain[:, None, :])
        specs.append(pl.BlockSpec((None, 1, k), lambda i, j: (norm_layer, 0, 0)))
    args.append(w)
    specs.append(pl.BlockSpec((k, tn), lambda i, j: (0, j)))
    if n_head_norm_tiles:
        args.append(head_gain)
        specs.append(pl.BlockSpec((1, tn), lambda i, j: (0, j)))
    if residual is not None:
        args.append(residual)
        specs.append(pl.BlockSpec((tm, tn), lambda i, j: (i, j)))
    body = functools.partial(_proj_body, pre_norm=pre_norm, n_head_norm_tiles=n_head_norm_tiles,
                             residual=residual is not None, tn=tn)
    return pl.pallas_call(
        body,
        grid=(m // tm, n // tn),
        in_specs=specs,
        out_specs=pl.BlockSpec((tm, tn), lambda i, j: (i, j)),
        out_shape=jax.ShapeDtypeStruct((m, n), F32),
        scratch_shapes=[pltpu.VMEM((tm, k), BF16)],
        compiler_params=_cp("parallel", "arbitrary"),
        name="proj",
    )(*args)


def _rmsnorm_body(x_ref, g_ref, o_ref):
    o_ref[...] = _rms(x_ref[...]) * g_ref[...]


def _rmsnorm(x, gains, layer, *, tm):
    m, d = x.shape
    return pl.pallas_call(
        _rmsnorm_body,
        grid=(m // tm,),
        in_specs=[pl.BlockSpec((tm, d), lambda i: (i, 0)),
                  pl.BlockSpec((None, 1, d), lambda i: (layer, 0, 0))],
        out_specs=pl.BlockSpec((tm, d), lambda i: (i, 0)),
        out_shape=jax.ShapeDtypeStruct((m, d), F32),
        compiler_params=_cp("parallel"),
        name="rmsnorm",
    )(x, gains[:, None, :])


def _moba_prompt_body(q_ref, k_ref, v_ref, o_ref, kmean_ref, *, nblk, blk, topk):
    qt = pl.program_id(2)
    scale = HEAD_DIM ** -0.5

    @pl.when(qt == 0)
    def _():
        for n in range(nblk):
            kmean_ref[n:n + 1, :] = jnp.mean(k_ref[n * blk:(n + 1) * blk, :], axis=0, keepdims=True)

    q = q_ref[...]
    gate = _dot_hi(q, kmean_ref[...], _dot_nt)
    col = lax.broadcasted_iota(I32, gate.shape, 1)
    colf = col.astype(F32)
    gate = jnp.where(col < qt, gate, NEG_INF)
    picked = jnp.zeros(gate.shape, F32)
    for _ in range(topk):
        best = jnp.max(gate, axis=-1, keepdims=True)
        first = jnp.min(jnp.where(gate == best, colf, float(nblk)), axis=-1, keepdims=True)
        hit = colf == first
        picked = jnp.where(hit, 1.0, picked)
        gate = jnp.where(hit, -jnp.inf, gate)
    picked = jnp.where(col < qt, picked, 0.0)

    qb = q.astype(BF16)
    rows = lax.broadcasted_iota(I32, (blk, blk), 0)
    cols = lax.broadcasted_iota(I32, (blk, blk), 1)

    def block(n):
        start = pl.multiple_of(n * blk, blk)
        kb = k_ref[pl.ds(start, blk), :].astype(BF16)
        vb = v_ref[pl.ds(start, blk), :].astype(BF16)
        return _dot_nt(qb, kb) * scale, vb

    s, vb = block(qt)
    s = jnp.where(cols <= rows, s, NEG_INF)
    m0 = jnp.max(s, axis=-1, keepdims=True)
    p = jnp.exp(s - m0)
    l0 = jnp.sum(p, axis=-1, keepdims=True)
    acc0 = _dot(p.astype(BF16), vb)

    def body(n, carry):
        m, l, acc = carry
        s, vb = block(n)
        chosen = jnp.sum(jnp.where(col == n, picked, 0.0), axis=-1, keepdims=True)
        s = jnp.where(chosen > 0.0, s, NEG_INF)
        m_new = jnp.maximum(m, jnp.max(s, axis=-1, keepdims=True))
        a = jnp.exp(m - m_new)
        p = jnp.exp(s - m_new)
        return (m_new, a * l + jnp.sum(p, axis=-1, keepdims=True),
                a * acc + _dot(p.astype(BF16), vb))

    _, l, acc = lax.fori_loop(0, qt, body, (m0, l0, acc0))
    o_ref[...] = acc / l


def _moba_prompt_attn(qkv, batch, seq, heads):
    blk = MOBA_BLOCK
    nblk = seq // blk
    topk = min(MOBA_TOPK, (seq - 1) // blk)
    body = functools.partial(_moba_prompt_body, nblk=nblk, blk=blk, topk=topk)
    return pl.pallas_call(
        body,
        grid=(batch, heads, nblk),
        in_specs=[
            pl.BlockSpec((blk, HEAD_DIM), lambda b, h, t: (b * nblk + t, h)),
            pl.BlockSpec((seq, HEAD_DIM), lambda b, h, t: (b, heads + h)),
            pl.BlockSpec((seq, HEAD_DIM), lambda b, h, t: (b, 2 * heads + h)),
        ],
        out_specs=pl.BlockSpec((blk, HEAD_DIM), lambda b, h, t: (b * nblk + t, h)),
        out_shape=jax.ShapeDtypeStruct((batch * seq, heads * HEAD_DIM), F32),
        scratch_shapes=[pltpu.VMEM((nblk, HEAD_DIM), F32)],
        compiler_params=_cp("parallel", "parallel", "arbitrary"),
        name="moba_prompt",
    )(qkv, qkv, qkv)


def _moba_gate_body(pt_ref, k0_ref, k1_ref, q_ref, e_ref, o_ref, km_ref, *, nfull, topk):
    n = pl.program_id(1)
    tot = jnp.sum(k0_ref[...], axis=0, keepdims=True) + jnp.sum(k1_ref[...], axis=0, keepdims=True)
    km_ref[pl.ds(n, 1), :] = tot / float(MOBA_BLOCK)

    @pl.when(n == nfull - 1)
    def _():
        prod = km_ref[...] * q_ref[...]
        gate = _dot_exact_rhs(prod, e_ref[...])
        rowf = lax.broadcasted_iota(I32, gate.shape, 0).astype(F32)
        rank = lax.broadcasted_iota(I32, o_ref.shape, 0)
        picks = jnp.zeros(o_ref.shape, F32)
        for r in range(topk):
            best = jnp.max(gate, axis=0, keepdims=True)
            first = jnp.min(jnp.where(gate == best, rowf, float(nfull)), axis=0, keepdims=True)
            picks = jnp.where(rank == r, first, picks)
            gate = jnp.where(rowf == first, -jnp.inf, gate)
        o_ref[...] = picks.astype(I32)


def _moba_sample_select(pool_k2, q3, page_table_flat, nfull, heads):
    bsz = q3.shape[0]
    width = heads * HEAD_DIM
    n_pages = page_table_flat.shape[0] // bsz
    topk = min(MOBA_TOPK, nfull)
    e = (jnp.arange(width)[:, None] // HEAD_DIM == jnp.arange(LANES)[None, :]).astype(BF16)
    body = functools.partial(_moba_gate_body, nfull=nfull, topk=topk)
    pages_per_block = MOBA_BLOCK // PAGE_SIZE
    assert pages_per_block == 2
    grid_spec = pltpu.PrefetchScalarGridSpec(
        num_scalar_prefetch=1,
        grid=(bsz, nfull),
        in_specs=[
            pl.BlockSpec((None, PAGE_SIZE, width), lambda b, n, pt: (pt[b * n_pages + 2 * n], 0, 0)),
            pl.BlockSpec((None, PAGE_SIZE, width), lambda b, n, pt: (pt[b * n_pages + 2 * n + 1], 0, 0)),
            pl.BlockSpec((None, 1, width), lambda b, n, pt: (b, 0, 0)),
            pl.BlockSpec((width, LANES), lambda b, n, pt: (0, 0)),
        ],
        out_specs=pl.BlockSpec((None, 8, LANES), lambda b, n, pt: (b, 0, 0)),
        scratch_shapes=[pltpu.VMEM((nfull, width), F32)],
    )
    return pl.pallas_call(
        body,
        grid_spec=grid_spec,
        out_shape=jax.ShapeDtypeStruct((bsz, 8, LANES), I32),
        compiler_params=_cp("parallel", "arbitrary"),
        name="moba_sample_select",
    )(page_table_flat, pool_k2, pool_k2, q3, e)


def _moba_sample_attn_body(pt_ref, sel_ref, q_ref, kn_ref, vn_ref, kp_ref, vp_ref, o_ref,
                           m_ref, l_ref, acc_ref, *, nsteps):
    r = pl.program_id(2)
    scale = HEAD_DIM ** -0.5
    q8 = jnp.broadcast_to(q_ref[...], (8, HEAD_DIM))

    @pl.when(r == 0)
    def _():
        s_own = jnp.sum(q8 * kn_ref[...], axis=-1, keepdims=True) * scale
        m_ref[...] = jnp.broadcast_to(s_own, m_ref.shape)
        l_ref[...] = jnp.ones(l_ref.shape, F32)
        acc_ref[...] = jnp.broadcast_to(vn_ref[...], acc_ref.shape)

    s = _dot_nt(q8.astype(BF16), kp_ref[...].astype(BF16)) * scale
    m = m_ref[...]
    m_new = jnp.maximum(m, jnp.max(s, axis=-1, keepdims=True))
    a = jnp.exp(m - m_new)
    p = jnp.exp(s - m_new)
    l_new = a * l_ref[...] + jnp.sum(p, axis=-1, keepdims=True)
    acc_new = a * acc_ref[...] + _dot(p.astype(BF16), vp_ref[...].astype(BF16))
    m_ref[...] = m_new
    l_ref[...] = l_new
    acc_ref[...] = acc_new

    @pl.when(r == nsteps - 1)
    def _():
        o_ref[...] = (acc_new / l_new)[0:1, :]


def _moba_sample_attn(qkv3, pool_k2, pool_v2, page_table_flat, sel_flat, heads, topk):
    bsz = qkv3.shape[0]
    n_pages = page_table_flat.shape[0] // bsz
    nsteps = topk * 2

    def page(b, h, r, pt, sel):
        blk = sel[(b * topk + r // 2) * heads + h]
        return pt[b * n_pages + 2 * blk + r % 2], 0, h

    grid_spec = pltpu.PrefetchScalarGridSpec(
        num_scalar_prefetch=2,
        grid=(bsz, heads, nsteps),
        in_specs=[
            pl.BlockSpec((None, 1, HEAD_DIM), lambda b, h, r, pt, sel: (b, 0, h)),
            pl.BlockSpec((None, 1, HEAD_DIM), lambda b, h, r, pt, sel: (b, 0, heads + h)),
            pl.BlockSpec((None, 1, HEAD_DIM), lambda b, h, r, pt, sel: (b, 0, 2 * heads + h)),
            pl.BlockSpec((None, PAGE_SIZE, HEAD_DIM), page),
            pl.BlockSpec((None, PAGE_SIZE, HEAD_DIM), page),
        ],
        out_specs=pl.BlockSpec((None, 1, HEAD_DIM), lambda b, h, r, pt, sel: (b, 0, h)),
        scratch_shapes=[pltpu.VMEM((8, LANES), F32), pltpu.VMEM((8, LANES), F32),
                        pltpu.VMEM((8, HEAD_DIM), F32)],
    )
    return pl.pallas_call(
        functools.partial(_moba_sample_attn_body, nsteps=nsteps),
        grid_spec=grid_spec,
        out_shape=jax.ShapeDtypeStruct((bsz, 1, heads * HEAD_DIM), F32),
        compiler_params=_cp("parallel", "parallel", "arbitrary"),
        name="moba_sample_attn",
    )(page_table_flat, sel_flat, qkv3, qkv3, qkv3, pool_k2, pool_v2)


def _pool_body(h_ref, prev_ref, w_ref, sc_ref, x_ref, o_ref, *, tm, tiles_per_seq, group):
    tile = pl.program_id(0) % tiles_per_seq
    h = h_ref[...]
    prev = jnp.where(tile == 0, 0.0, prev_ref[...])
    ext = jnp.concatenate([prev, h], axis=0)
    pos = tile * tm + lax.broadcasted_iota(I32, (tm, 1), 0)
    for g, w in enumerate(POOL_WINDOWS):
        sl = slice(g * group, (g + 1) * group)
        cur = ext[:, sl]
        shift = 1
        while shift < w:
            cur = cur + pltpu.roll(cur, shift, 0)
            shift *= 2
        cnt = jnp.minimum(pos + 1, w).astype(F32)
        pooled = cur[POOL_HALO:, :] / cnt - h[:, sl]
        y = _dot(pooled.astype(BF16), w_ref[g].astype(BF16))
        o_ref[:, sl] = x_ref[:, sl] + y * sc_ref[:, sl]


def _pool_mix(h, x, w_pool, scale, seq, *, tm):
    m, d = h.shape
    group = d // len(POOL_WINDOWS)
    tiles_per_seq = seq // tm
    halo_blocks = tm // POOL_HALO
    body = functools.partial(_pool_body, tm=tm, tiles_per_seq=tiles_per_seq, group=group)
    return pl.pallas_call(
        body,
        grid=(m // tm,),
        in_specs=[
            pl.BlockSpec((tm, d), lambda i: (i, 0)),
            pl.BlockSpec((POOL_HALO, d), lambda i: (jnp.maximum(i * halo_blocks - 1, 0), 0)),
            pl.BlockSpec((len(POOL_WINDOWS), group, group), lambda i: (0, 0, 0)),
            pl.BlockSpec((1, d), lambda i: (0, 0)),
            pl.BlockSpec((tm, d), lambda i: (i, 0)),
        ],
        out_specs=pl.BlockSpec((tm, d), lambda i: (i, 0)),
        out_shape=jax.ShapeDtypeStruct((m, d), F32),
        compiler_params=_cp("parallel"),
        name="pool_mix",
    )(h, h, w_pool, scale, x)


def _float_key(score):
    bits = lax.bitcast_convert_type(score, I32)
    return jnp.where(bits < 0, bits ^ 0x7FFFFFFF, bits)


def _count(mask, axes):
    ones = jnp.where(mask, 1.0, 0.0)
    for ax in (axes if isinstance(axes, tuple) else (axes,)):
        ones = jnp.sum(ones, axis=ax, keepdims=True)
    return ones


def _kth_largest_key(count_ge, k, shape):
    t = jnp.where(count_ge(jnp.zeros(shape, I32)) >= k, 0, INT_MIN).astype(I32)

    def body(i, t):
        cand = t | jnp.left_shift(jnp.int32(1), 30 - i)
        return jnp.where(count_ge(cand) >= k, cand, t)

    return lax.fori_loop(0, 31, body, t)


def _tie_limit(count_eq_below, need, shape, nbits):
    def body(i, j):
        cand = j | jnp.left_shift(jnp.int32(1), nbits - 1 - i)
        return jnp.where(count_eq_below(cand) <= need, cand, j)

    return lax.fori_loop(0, nbits, body, jnp.zeros(shape, I32))


def _dsa_prompt_body(q_ref, k_ref, v_ref, qi_ref, wq_ref, ki_ref, o_ref,
                     qs_ref, os_ref, mask_ref, lim_ref, *, seq, tq, ntop, heads):
    qt = pl.program_id(1)
    scale = HEAD_DIM ** -0.5
    group = heads // C_KV_HEADS

    qi = qi_ref[...]
    wi = wq_ref[:, IDX_DIM:IDX_DIM + IDX_HEADS] * (IDX_HEADS ** -0.5)
    kib = ki_ref[...]
    score = jnp.zeros((tq, seq), F32)
    for h in range(IDX_HEADS):
        d = _dot_nt(qi[:, h * IDX_DIM:(h + 1) * IDX_DIM].astype(BF16), kib) * (IDX_DIM ** -0.5)
        score = score + jnp.maximum(d, 0.0) * wi[:, h:h + 1]
    kpos = lax.broadcasted_iota(I32, (tq, seq), 1)
    tpos = qt * tq + lax.broadcasted_iota(I32, (tq, seq), 0)
    causal = kpos <= tpos
    key = _float_key(jnp.where(causal, score, NEG_INF))

    thr = _kth_largest_key(lambda c: _count(key >= c, -1), float(ntop), (tq, 1))
    above = _count(key > thr, -1)
    tied = key == thr
    need = float(ntop) - above
    lim_ref[...] = jnp.full((tq, 1), seq, I32)

    @pl.when(jnp.max(_count(tied, -1) - need) > 0.0)
    def _():
        lim_ref[...] = _tie_limit(lambda j: _count(tied & (kpos < j), -1), need, (tq, 1),
                                  seq.bit_length())

    chosen = (key > thr) | (tied & (kpos < lim_ref[...]))
    mask_ref[...] = jnp.where(chosen & causal, 1.0, 0.0)

    for h in range(heads):
        qs_ref[h] = q_ref[:, h * HEAD_DIM:(h + 1) * HEAD_DIM].astype(BF16)
    for g in range(C_KV_HEADS):
        sl = slice(g * HEAD_DIM, (g + 1) * HEAD_DIM)

        def head(j, carry, g=g, sl=sl):
            h = g * group + j
            s = _dot_nt(qs_ref[h], k_ref[:, sl]) * scale
            s = jnp.where(mask_ref[...] > 0.0, s, NEG_INF)
            p = jnp.exp(s - jnp.max(s, axis=-1, keepdims=True))
            l = jnp.sum(p, axis=-1, keepdims=True)
            os_ref[h] = _dot(p.astype(BF16), v_ref[:, sl]) / l
            return carry

        lax.fori_loop(0, group, head, 0)
    for h in range(heads):
        o_ref[:, h * HEAD_DIM:(h + 1) * HEAD_DIM] = os_ref[h]


def _dsa_prompt_attn(proj, tail, k_bf, v_bf, ki_bf, batch, seq, heads):
    tq = DSA_Q_TILE
    nt = seq // tq
    ntop = min(DSA_TOPK, seq // 4)
    qw = heads * HEAD_DIM
    kvw = C_KV_HEADS * HEAD_DIM
    iw = IDX_HEADS * IDX_DIM
    assert (qw + 2 * kvw) % iw == 0
    body = functools.partial(_dsa_prompt_body, seq=seq, tq=tq, ntop=ntop, heads=heads)
    return pl.pallas_call(
        body,
        grid=(batch, nt),
        in_specs=[
            pl.BlockSpec((tq, qw), lambda b, t: (b * nt + t, 0)),
            pl.BlockSpec((seq, kvw), lambda b, t: (b, 0)),
            pl.BlockSpec((seq, kvw), lambda b, t: (b, 0)),
            pl.BlockSpec((tq, iw), lambda b, t: (b * nt + t, (qw + 2 * kvw) // iw)),
            pl.BlockSpec((tq, tail.shape[1]), lambda b, t: (b * nt + t, 0)),
            pl.BlockSpec((seq, IDX_DIM), lambda b, t: (b, 0)),
        ],
        out_specs=pl.BlockSpec((tq, qw), lambda b, t: (b * nt + t, 0)),
        out_shape=jax.ShapeDtypeStruct((batch * seq, qw), F32),
        scratch_shapes=[pltpu.VMEM((heads, tq, HEAD_DIM), BF16), pltpu.VMEM((heads, tq, HEAD_DIM), F32),
                        pltpu.VMEM((tq, seq), F32), pltpu.VMEM((tq, 1), I32)],
        compiler_params=_cp("parallel", "arbitrary"),
        name="dsa_prompt",
    )(proj, k_bf, v_bf, proj, tail, ki_bf)


def _index_score_rows(qi, wi, ki):
    d = _dot_nt(qi.astype(BF16), ki.astype(BF16)) * (IDX_DIM ** -0.5)
    return jnp.sum(jnp.maximum(d, 0.0) * (wi * (IDX_HEADS ** -0.5)), axis=0, keepdims=True)


def _dsa_sample_scores_body(pt_ref, *refs, bsz):
    ki_refs = refs[:bsz]
    qi_ref, wi_ref, o_ref = refs[bsz:]
    p = pl.program_id(0)
    for b in range(bsz):
        o_ref[b, pl.ds(p, 1), :] = _index_score_rows(qi_ref[b], wi_ref[b], ki_refs[b][...])


def _dsa_sample_scores(pool_ki, qi3, wi3, page_table_flat):
    bsz = qi3.shape[0]
    n_pages = page_table_flat.shape[0] // bsz
    page_specs = [
        pl.BlockSpec((None, PAGE_SIZE, IDX_DIM), lambda p, pt, b=b: (pt[b * n_pages + p], 0, 0))
        for b in range(bsz)
    ]
    grid_spec = pltpu.PrefetchScalarGridSpec(
        num_scalar_prefetch=1,
        grid=(n_pages,),
        in_specs=page_specs + [
            pl.BlockSpec(qi3.shape, lambda p, pt: (0, 0, 0)),
            pl.BlockSpec(wi3.shape, lambda p, pt: (0, 0, 0)),
        ],
        out_specs=pl.BlockSpec((bsz, n_pages, PAGE_SIZE), lambda p, pt: (0, 0, 0)),
    )
    return pl.pallas_call(
        functools.partial(_dsa_sample_scores_body, bsz=bsz),
        grid_spec=grid_spec,
        out_shape=jax.ShapeDtypeStruct((bsz, n_pages, PAGE_SIZE), F32),
        compiler_params=_cp("arbitrary"),
        name="dsa_sample_scores",
    )(page_table_flat, *([pool_ki] * bsz), qi3, wi3)


def _dsa_sample_attn_body(pt_ref, sc_ref, qi_ref, wi_ref, kin_ref, q_ref, kn_ref, vn_ref, kp_ref, vp_ref,
                          o_ref, mask_ref, new_ref, m_ref, l_ref, acc_ref, *, n_pages, ntop, heads):
    p = pl.program_id(1)
    scale = HEAD_DIM ** -0.5
    group = heads // C_KV_HEADS
    past = n_pages * PAGE_SIZE

    @pl.when(p == 0)
    def _():
        key = _float_key(sc_ref[...])
        d_new = jnp.sum(qi_ref[...] * kin_ref[...], axis=-1, keepdims=True) * (IDX_DIM ** -0.5)
        key_new = _float_key(jnp.sum(jnp.maximum(d_new, 0.0) * (wi_ref[...] * (IDX_HEADS ** -0.5)),
                                     axis=0, keepdims=True))
        pos = (lax.broadcasted_iota(I32, key.shape, 0) * PAGE_SIZE
               + lax.broadcasted_iota(I32, key.shape, 1))

        def count_ge(c):
            return _count(key >= c, (0, 1)) + jnp.where(key_new >= c, 1.0, 0.0)

        thr = _kth_largest_key(count_ge, float(ntop), (1, 1))
        above = _count(key > thr, (0, 1)) + jnp.where(key_new > thr, 1.0, 0.0)
        tied = key == thr
        tied_new = key_new == thr
        need = float(ntop) - above

        def count_eq_below(j):
            return _count(tied & (pos < j), (0, 1)) + jnp.where(tied_new & (past < j), 1.0, 0.0)

        lim = _tie_limit(count_eq_below, need, (1, 1), (past + 1).bit_length())
        mask_ref[...] = jnp.where((key > thr) | (tied & (pos < lim)), 1.0, 0.0)
        new_ref[...] = jnp.broadcast_to(
            jnp.where((key_new > thr) | (tied_new & (past < lim)), 1.0, 0.0), new_ref.shape)
        m_ref[...] = jnp.full(m_ref.shape, NEG_INF, F32)
        l_ref[...] = jnp.zeros(l_ref.shape, F32)
        acc_ref[...] = jnp.zeros(acc_ref.shape, F32)

    q = q_ref[...]
    qb = q.astype(BF16)
    kv_head = lax.broadcasted_iota(I32, (heads, 1), 0) // group

    def per_kv_head(f):
        out = f(0)
        for g in range(1, C_KV_HEADS):
            out = jnp.where(kv_head == g, f(g), out)
        return out

    s = per_kv_head(lambda g: _dot_nt(
        qb, kp_ref[:, g * HEAD_DIM:(g + 1) * HEAD_DIM].astype(BF16))) * scale
    s = jnp.where(mask_ref[pl.ds(p, 1), :] > 0.0, s, NEG_INF)
    m = m_ref[...]
    m_new = jnp.maximum(m, jnp.max(s, axis=-1, keepdims=True))
    a = jnp.exp(m - m_new)
    pr = jnp.where(s > NEG_INF, jnp.exp(s - m_new), 0.0)
    prb = pr.astype(BF16)
    pv = per_kv_head(lambda g: _dot(
        prb, vp_ref[:, g * HEAD_DIM:(g + 1) * HEAD_DIM].astype(BF16)))
    l_new = a * l_ref[...] + jnp.sum(pr, axis=-1, keepdims=True)
    acc_new = a * acc_ref[...] + pv
    m_ref[...] = m_new
    l_ref[...] = l_new
    acc_ref[...] = acc_new

    @pl.when(p == n_pages - 1)
    def _():
        s_new = jnp.sum(q * kn_ref[...], axis=-1, keepdims=True) * scale
        take = new_ref[:, 0:1] > 0.0
        s_new = jnp.where(take, s_new, NEG_INF)
        m_fin = jnp.maximum(m_new, s_new)
        a2 = jnp.exp(m_new - m_fin)
        p_new = jnp.where(take, jnp.exp(s_new - m_fin), 0.0)
        l_fin = a2 * l_new + p_new
        o_ref[...] = (a2 * acc_new + p_new * vn_ref[...]) / l_fin


def _dsa_sample_attn(scores, qi3, wi3, kin3, q3, kn3, vn3, pool_k2, pool_v2, page_table_flat, heads):
    bsz, n_pages, _ = scores.shape
    ntop = min(DSA_TOPK, (n_pages * PAGE_SIZE + 1) // 4)
    kvw = C_KV_HEADS * HEAD_DIM
    per_b = lambda shape: pl.BlockSpec((None,) + shape, lambda b, p, pt: (b, 0, 0))
    page = pl.BlockSpec((None, PAGE_SIZE, kvw), lambda b, p, pt: (pt[b * n_pages + p], 0, 0))
    grid_spec = pltpu.PrefetchScalarGridSpec(
        num_scalar_prefetch=1,
        grid=(bsz, n_pages),
        in_specs=[per_b((n_pages, PAGE_SIZE)), per_b((IDX_HEADS, IDX_DIM)), per_b((IDX_HEADS, 1)),
                  per_b((1, IDX_DIM)), per_b((heads, HEAD_DIM)), per_b((heads, HEAD_DIM)),
                  per_b((heads, HEAD_DIM)), page, page],
        out_specs=per_b((heads, HEAD_DIM)),
        scratch_shapes=[pltpu.VMEM((n_pages, PAGE_SIZE), F32), pltpu.VMEM((heads, LANES), F32),
                        pltpu.VMEM((heads, 1), F32), pltpu.VMEM((heads, 1), F32),
                        pltpu.VMEM((heads, HEAD_DIM), F32)],
    )
    body = functools.partial(_dsa_sample_attn_body, n_pages=n_pages, ntop=ntop, heads=heads)
    return pl.pallas_call(
        body,
        grid_spec=grid_spec,
        out_shape=jax.ShapeDtypeStruct((bsz, heads, HEAD_DIM), F32),
        compiler_params=_cp("parallel", "arbitrary"),
        name="dsa_sample_attn",
    )(page_table_flat, scores, qi3, wi3, kin3, q3, kn3, vn3, pool_k2, pool_v2)


def _gla_body(q_ref, k_ref, v_ref, r_ref, low_ref, wg2_ref, bg_ref, gn_ref, s0_ref, o_ref, sf_ref,
              state_ref, k_s, b_s, *, chunk, n_valid, dk, dv):
    c = pl.program_id(2)

    @pl.when(c == 0)
    def _():
        state_ref[...] = s0_ref[...]

    x = _dot_hi(low_ref[...], wg2_ref[...]) + bg_ref[...]
    g = (jnp.minimum(x, 0.0) - jnp.log(1.0 + jnp.exp(-jnp.abs(x)))) / GLA_TAU
    row = lax.broadcasted_iota(I32, (chunk, 1), 0)
    if n_valid < chunk:
        g = jnp.where(row < n_valid, g, 0.0)
    tri = (lax.broadcasted_iota(I32, (chunk, chunk), 1)
           <= lax.broadcasted_iota(I32, (chunk, chunk), 0)).astype(BF16)
    bcum = _dot_exact_lhs(tri, g)
    q = q_ref[...] * (dk ** -0.5)
    k = k_ref[...]
    vb = v_ref[...].astype(BF16)
    k_s[...] = k
    b_s[...] = bcum
    lane = lax.broadcasted_iota(I32, (chunk, chunk), 1)

    def column(s, att):
        ks = k_s[pl.ds(s, 1), :]
        bs = b_s[pl.ds(s, 1), :]
        decay = jnp.exp(jnp.where(row >= s, bcum - bs, -jnp.inf))
        col = jnp.sum(q * ks * decay, axis=-1, keepdims=True)
        return jnp.where(lane == s, col, att)

    att = lax.fori_loop(0, chunk, column, jnp.zeros((chunk, chunk), F32))
    state = state_ref[...]
    o = _dot(att.astype(BF16), vb) + _dot((q * jnp.exp(bcum)).astype(BF16), state.astype(BF16))

    b_last = bcum[chunk - 1:chunk, :]
    last_row = (lax.broadcasted_iota(I32, (chunk, dv), 0) == chunk - 1).astype(BF16)
    b_last_cols = _dot_exact_rhs(bcum, last_row, _dot_tn)
    kd = (k * jnp.exp(b_last - bcum)).astype(BF16)
    new_state = jnp.exp(b_last_cols) * state + _dot_tn(kd, vb)
    state_ref[...] = new_state

    @pl.when(c == pl.num_programs(2) - 1)
    def _():
        sf_ref[...] = new_state

    o_ref[...] = _rms(o) * gn_ref[...] * _silu(r_ref[...])


def _gla(proj3, low3, w_g2, b_g, g_n, s0, *, chunk, n_valid):
    bsz, length, _ = proj3.shape
    heads = GLA_HEADS
    dk, dv = s0.shape[2], s0.shape[3]
    assert dv == 2 * dk
    body = functools.partial(_gla_body, chunk=chunk, n_valid=n_valid, dk=dk, dv=dv)
    return pl.pallas_call(
        body,
        grid=(bsz, heads, length // chunk),
        in_specs=[
            pl.BlockSpec((None, chunk, dk), lambda b, h, c: (b, c, h)),
            pl.BlockSpec((None, chunk, dk), lambda b, h, c: (b, c, heads + h)),
            pl.BlockSpec((None, chunk, dv), lambda b, h, c: (b, c, heads + h)),
            pl.BlockSpec((None, chunk, dv), lambda b, h, c: (b, c, 2 * heads + h)),
            pl.BlockSpec((None, chunk, low3.shape[2]), lambda b, h, c: (b, c, 0)),
            pl.BlockSpec((low3.shape[2], dk), lambda b, h, c: (0, h)),
            pl.BlockSpec((1, dk), lambda b, h, c: (0, h)),
            pl.BlockSpec((1, dv), lambda b, h, c: (0, 0)),
            pl.BlockSpec((None, None, dk, dv), lambda b, h, c: (b, h, 0, 0)),
        ],
        out_specs=[
            pl.BlockSpec((None, chunk, dv), lambda b, h, c: (b, c, h)),
            pl.BlockSpec((None, None, dk, dv), lambda b, h, c: (b, h, 0, 0)),
        ],
        out_shape=[jax.ShapeDtypeStruct((bsz, length, heads * dv), F32),
                   jax.ShapeDtypeStruct(s0.shape, F32)],
        scratch_shapes=[pltpu.VMEM((dk, dv), F32), pltpu.VMEM((chunk, dk), F32),
                        pltpu.VMEM((chunk, dk), F32)],
        compiler_params=_cp("parallel", "parallel", "arbitrary"),
        name="gla",
    )(proj3, proj3, proj3, proj3, low3, w_g2, b_g, g_n, s0)


def _row_tile(m):
    return 512 if m % 512 == 0 else m


def _head_gain_row(g, n_heads, total):
    return jnp.concatenate([jnp.tile(g, n_heads), jnp.ones((total - n_heads * g.shape[0],), F32)])[None, :]


def _mixer_a(xp, xs, batch, seq, norm_mix, layer, cache_k, cache_v, page_table, w_qkv, w_o, g_q, g_k):
    d = xp.shape[1]
    heads = d // HEAD_DIM
    gain = jnp.concatenate([jnp.tile(g_q, heads), jnp.tile(g_k, heads), jnp.ones((d,), F32)])[None, :]
    tn = 512

    def qkv(x):
        return _proj(x, w_qkv, tm=_row_tile(x.shape[0]), tn=tn, norm_gain=norm_mix, norm_layer=layer,
                     head_gain=gain, n_head_norm_tiles=2 * d // tn)

    qkv_p = qkv(xp)
    o_p = _moba_prompt_attn(qkv_p, batch, seq, heads)
    xp = _proj(o_p, w_o, tm=_row_tile(xp.shape[0]), tn=tn, residual=xp)

    bsz = xs.shape[0]
    n_pages = page_table.shape[1]
    past = n_pages * PAGE_SIZE
    assert past % MOBA_BLOCK == 0
    nfull = past // MOBA_BLOCK
    topk = min(MOBA_TOPK, nfull)
    assert topk > 0
    qkv_s = qkv(xs)
    qkv_s3 = qkv_s[:, None, :]
    pool_k2 = cache_k.reshape(cache_k.shape[0], PAGE_SIZE, d)
    pool_v2 = cache_v.reshape(cache_v.shape[0], PAGE_SIZE, d)
    pt_flat = page_table.reshape(-1)
    sel = _moba_sample_select(pool_k2, qkv_s3[:, :, :d], pt_flat, nfull, heads)
    sel_flat = sel[:, :topk, :heads].reshape(-1)
    o_s = _moba_sample_attn(qkv_s3, pool_k2, pool_v2, pt_flat, sel_flat, heads, topk)
    xs = _proj(o_s[:, 0, :], w_o, tm=bsz, tn=tn, residual=xs)

    kv_shape = lambda t, n: t.reshape(n, -1, heads, HEAD_DIM)
    outs = (kv_shape(qkv_p[:, d:2 * d], batch), kv_shape(qkv_p[:, 2 * d:], batch),
            kv_shape(qkv_s[:, d:2 * d], bsz), kv_shape(qkv_s[:, 2 * d:], bsz))
    return xp, xs, outs


def _mixer_b(xp, xs, batch, seq, norm_mix, layer, state_pool, w_pool, scale):
    d = xp.shape[1]
    keep = state_pool.shape[1]
    scale = scale[None, :]
    hp = _rmsnorm(xp, norm_mix, layer, tm=_row_tile(xp.shape[0]))
    xp = _pool_mix(hp, xp, w_pool, scale, seq, tm=512)
    pool_p = hp.reshape(batch, seq, d)[:, seq - keep:]

    bsz = xs.shape[0]
    hs = _rmsnorm(xs, norm_mix, layer, tm=bsz)
    ext = jnp.concatenate([state_pool, hs[:, None, :]], axis=1)
    assert keep + 1 == POOL_HALO
    x_ext = jnp.concatenate([jnp.zeros_like(state_pool), xs[:, None, :]], axis=1)
    y = _pool_mix(ext.reshape(bsz * POOL_HALO, d), x_ext.reshape(bsz * POOL_HALO, d), w_pool, scale,
                  POOL_HALO, tm=POOL_HALO)
    xs = y.reshape(bsz, POOL_HALO, d)[:, -1]
    return xp, xs, (pool_p, ext[:, 1:])


def _mixer_c(xp, xs, batch, seq, norm_mix, layer, cache_k, cache_v, cache_ki, page_table,
             w_in, w_o, g_q, g_k):
    d = xp.shape[1]
    heads = d // HEAD_DIM
    kvw = C_KV_HEADS * HEAD_DIM
    iw = IDX_HEADS * IDX_DIM
    main = d + 2 * kvw + iw
    w_main, w_tail = w_in[:, :main], w_in[:, main:]
    gain = jnp.concatenate([jnp.tile(g_q, heads), jnp.tile(g_k, C_KV_HEADS),
                            jnp.ones((main - d - kvw,), F32)])[None, :]
    tn = 512

    def project(x):
        tm = _row_tile(x.shape[0])
        pm = _proj(x, w_main, tm=tm, tn=tn, norm_gain=norm_mix, norm_layer=layer, head_gain=gain,
                   n_head_norm_tiles=(d + kvw) // tn)
        pt = _proj(x, w_tail, tm=tm, tn=w_tail.shape[1], norm_gain=norm_mix, norm_layer=layer)
        return pm, pt

    pm, pt = project(xp)
    k_p, v_p, ki_p = pm[:, d:d + kvw], pm[:, d + kvw:d + 2 * kvw], pt[:, :IDX_DIM]
    o_p = _dsa_prompt_attn(pm, pt, k_p.astype(BF16), v_p.astype(BF16), ki_p.astype(BF16), batch, seq, heads)
    xp = _proj(o_p, w_o, tm=_row_tile(xp.shape[0]), tn=tn, residual=xp)

    bsz = xs.shape[0]
    group = heads // C_KV_HEADS
    sm, st = project(xs)
    k_s, v_s, ki_s = sm[:, d:d + kvw], sm[:, d + kvw:d + 2 * kvw], st[:, :IDX_DIM]
    qi3 = sm[:, d + 2 * kvw:].reshape(bsz, IDX_HEADS, IDX_DIM)
    wi3 = st[:, IDX_DIM:IDX_DIM + IDX_HEADS].reshape(bsz, IDX_HEADS, 1)
    pt_flat = page_table.reshape(-1)
    scores = _dsa_sample_scores(cache_ki, qi3, wi3, pt_flat)
    per_q_head = lambda t: jnp.repeat(t.reshape(bsz, C_KV_HEADS, HEAD_DIM), group, axis=1)
    o_s = _dsa_sample_attn(scores, qi3, wi3, ki_s[:, None, :], sm[:, :d].reshape(bsz, heads, HEAD_DIM),
                           per_q_head(k_s), per_q_head(v_s),
                           cache_k.reshape(cache_k.shape[0], PAGE_SIZE, kvw),
                           cache_v.reshape(cache_v.shape[0], PAGE_SIZE, kvw), pt_flat, heads)
    xs = _proj(o_s.reshape(bsz, d), w_o, tm=bsz, tn=tn, residual=xs)

    kv4 = lambda t, n: t.reshape(n, -1, C_KV_HEADS, HEAD_DIM)
    outs = (kv4(k_p, batch), kv4(v_p, batch), ki_p.reshape(batch, seq, IDX_DIM),
            kv4(k_s, bsz), kv4(v_s, bsz), ki_s.reshape(bsz, 1, IDX_DIM))
    return xp, xs, outs


def _mixer_d(xp, xs, batch, seq, norm_mix, layer, state, w_in, w_g2, b_g, g_n, w_o):
    d = xp.shape[1]
    dk, dv = state.shape[2], state.shape[3]
    main = 2 * GLA_HEADS * dk + 2 * GLA_HEADS * dv
    w_main, w_tail = w_in[:, :main], w_in[:, main:]
    tn = 512

    def project(x):
        tm = _row_tile(x.shape[0])
        pm = _proj(x, w_main, tm=tm, tn=tn, norm_gain=norm_mix, norm_layer=layer)
        pt = _proj(x, w_tail, tm=tm, tn=w_tail.shape[1], norm_gain=norm_mix, norm_layer=layer)
        return pm, pt

    pm, pt = project(xp)
    zero_state = jnp.zeros((batch,) + state.shape[1:], F32)
    o_p, s_p = _gla(pm.reshape(batch, seq, main), pt.reshape(batch, seq, -1), w_g2, b_g[None, :],
                    g_n[None, :], zero_state, chunk=min(GLA_CHUNK, seq), n_valid=min(GLA_CHUNK, seq))
    xp = _proj(o_p.reshape(batch * seq, GLA_HEADS * dv), w_o, tm=_row_tile(xp.shape[0]), tn=tn, residual=xp)

    bsz = xs.shape[0]
    sm, st = project(xs)
    pad = lambda t: jnp.pad(t[:, None, :], ((0, 0), (0, GLA_SAMPLE_ROWS - 1), (0, 0)))
    o_s, s_s = _gla(pad(sm), pad(st), w_g2, b_g[None, :], g_n[None, :], state,
                    chunk=GLA_SAMPLE_ROWS, n_valid=1)
    xs = _proj(o_s[:, 0, :], w_o, tm=bsz, tn=tn, residual=xs)
    return xp, xs, (s_p, s_s)


def kernel(x_prompt, x_sample, cache_a_k, cache_a_v, state_b_pool, cache_c_k, cache_c_v, cache_c_idx_k,
           state_d_gla, page_table, norm_ffn1, ffn1_w_gate, ffn1_w_up, ffn1_w_down, norm_mix, norm_ffn2,
           ffn2_w_gate, ffn2_w_up, ffn2_w_down, a_w_qkv, a_w_o, a_g_q, a_g_k, b_w_pool, b_scale, c_w_in,
           c_w_o, c_g_q, c_g_k, d_w_in, d_w_g2, d_b_g, d_g_n, d_w_o):
    batch, seq, d = x_prompt.shape
    bsz = x_sample.shape[0]
    assert x_sample.shape[1] == 1
    xp = x_prompt.reshape(batch * seq, d)
    xs = x_sample.reshape(bsz, d)
    depth = norm_ffn1.shape[0]
    ffn_tile = 512
    outs = {}
    for i in range(depth):
        xp = _ffn(xp, norm_ffn1, ffn1_w_gate, ffn1_w_up, ffn1_w_down, i, tm=_row_tile(xp.shape[0]), tf=ffn_tile)
        xs = _ffn(xs, norm_ffn1, ffn1_w_gate, ffn1_w_up, ffn1_w_down, i, tm=bsz, tf=ffn_tile)
        m = i % 4
        if m == 0:
            xp, xs, outs["a"] = _mixer_a(xp, xs, batch, seq, norm_mix, i, cache_a_k, cache_a_v, page_table,
                                         a_w_qkv, a_w_o, a_g_q, a_g_k)
        elif m == 1:
            xp, xs, outs["b"] = _mixer_b(xp, xs, batch, seq, norm_mix, i, state_b_pool, b_w_pool, b_scale)
        elif m == 2:
            xp, xs, outs["c"] = _mixer_c(xp, xs, batch, seq, norm_mix, i, cache_c_k, cache_c_v, cache_c_idx_k,
                                         page_table, c_w_in, c_w_o, c_g_q, c_g_k)
        else:
            xp, xs, outs["d"] = _mixer_d(xp, xs, batch, seq, norm_mix, i, state_d_gla, d_w_in, d_w_g2,
                                         d_b_g, d_g_n, d_w_o)
        xp = _ffn(xp, norm_ffn2, ffn2_w_gate, ffn2_w_up, ffn2_w_down, i, tm=_row_tile(xp.shape[0]), tf=ffn_tile)
        xs = _ffn(xs, norm_ffn2, ffn2_w_gate, ffn2_w_up, ffn2_w_down, i, tm=bsz, tf=ffn_tile)
    return (xp.reshape(batch, seq, d), xs.reshape(bsz, 1, d)) + outs["a"] + outs["b"] + outs["c"] + outs["d"]
```

```python
import functools

import jax
import jax.numpy as jnp
from jax import lax
from jax.experimental import pallas as pl
from jax.experimental.pallas import tpu as pltpu

F32 = jnp.float32
BF16 = jnp.bfloat16
I32 = jnp.int32

NORM_EPS = 1e-6
NEG_INF = -1e30
HEAD_DIM = 128
LANES = 128
PAGE_SIZE = 128
MOBA_BLOCK = 256
MOBA_TOPK = 3
POOL_WINDOWS = (2, 4, 8, 16)
POOL_HALO = 16
C_KV_HEADS = 4
IDX_HEADS = 16
IDX_DIM = 64
DSA_TOPK = 256
DSA_Q_TILE = 128
GLA_HEADS = 4
GLA_GATE_RANK = 16
GLA_TAU = 16.0
GLA_CHUNK = 64
GLA_SAMPLE_ROWS = 16
VMEM_LIMIT = 56 * 1024 * 1024

INT_MIN = -2147483648


def _cp(*sem):
    return pltpu.CompilerParams(dimension_semantics=sem, vmem_limit_bytes=VMEM_LIMIT)


ROW_TILE = 1024
COL_TILE = 512


def _row_tile(m):
    return ROW_TILE if m % ROW_TILE == 0 else m


def _ffn_hidden_tile(tm):
    return 256 if tm >= ROW_TILE else 512


def _dot(a, b):
    return jnp.dot(a, b, preferred_element_type=F32)


def _dot_nt(a, b):
    return lax.dot_general(a, b, (((1,), (1,)), ((), ())), preferred_element_type=F32)


def _dot_tn(a, b):
    return lax.dot_general(a, b, (((0,), (0,)), ((), ())), preferred_element_type=F32)


def _split3(a):
    a1 = a.astype(BF16)
    r1 = a - a1.astype(F32)
    a2 = r1.astype(BF16)
    a3 = (r1 - a2.astype(F32)).astype(BF16)
    return a1, a2, a3


def _dot_hi(a, b, dot=_dot):
    a1, a2, _ = _split3(a)
    b1, b2, _ = _split3(b)
    return dot(a1, b1) + (dot(a1, b2) + dot(a2, b1))


def _dot_exact_lhs(a01, b, dot=_dot):
    b1, b2, b3 = _split3(b)
    return dot(a01, b1) + (dot(a01, b2) + dot(a01, b3))


def _dot_exact_rhs(a, b01, dot=_dot):
    a1, a2, a3 = _split3(a)
    return dot(a1, b01) + (dot(a2, b01) + dot(a3, b01))


def _rms(x):
    return x * lax.rsqrt(jnp.mean(x * x, axis=-1, keepdims=True) + NORM_EPS)


def _silu(x):
    return x * jax.nn.sigmoid(x)


def _ffn_body(x_ref, g_ref, wg_ref, wu_ref, wd_ref, o_ref, xn_ref):
    j = pl.program_id(1)

    @pl.when(j == 0)
    def _():
        xn_ref[...] = (_rms(x_ref[...]) * g_ref[...]).astype(BF16)

    xn = xn_ref[...]
    a = _dot(xn, wg_ref[...].astype(BF16))
    u = _dot(xn, wu_ref[...].astype(BF16))
    part = _dot((_silu(a) * u).astype(BF16), wd_ref[...].astype(BF16))

    @pl.when(j == 0)
    def _():
        o_ref[...] = part

    @pl.when(j > 0)
    def _():
        o_ref[...] += part

    @pl.when(j == pl.num_programs(1) - 1)
    def _():
        o_ref[...] = x_ref[...] + 0.5 * o_ref[...]


def _ffn(x, g, wg, wu, wd, layer):
    m, d = x.shape
    f = wg.shape[-1]
    tm = _row_tile(m)
    tf = _ffn_hidden_tile(tm)
    return pl.pallas_call(
        _ffn_body,
        grid=(m // tm, f // tf),
        in_specs=[
            pl.BlockSpec((tm, d), lambda i, j: (i, 0), pipeline_mode=pl.Buffered(1)),
            pl.BlockSpec((None, 1, d), lambda i, j: (layer, 0, 0)),
            pl.BlockSpec((None, d, tf), lambda i, j: (layer, 0, j)),
            pl.BlockSpec((None, d, tf), lambda i, j: (layer, 0, j)),
            pl.BlockSpec((None, tf, d), lambda i, j: (layer, j, 0)),
        ],
        out_specs=pl.BlockSpec((tm, d), lambda i, j: (i, 0)),
        out_shape=jax.ShapeDtypeStruct((m, d), F32),
        scratch_shapes=[pltpu.VMEM((tm, d), BF16)],
        compiler_params=_cp("parallel", "arbitrary"),
        name="ffn",
    )(x, g[:, None, :], wg, wu, wd)


def _proj_body(*refs, pre_norm, n_head_norm_tiles, residual, tn):
    it = iter(refs)
    x_ref = next(it)
    g_ref = next(it) if pre_norm else None
    w_ref = next(it)
    cg_ref = next(it) if n_head_norm_tiles else None
    r_ref = next(it) if residual else None
    o_ref = next(it)
    xn_ref = next(it)
    j = pl.program_id(1)

    @pl.when(j == 0)
    def _():
        x = x_ref[...]
        if pre_norm:
            x = _rms(x) * g_ref[...]
        xn_ref[...] = x.astype(BF16)

    y = _dot(xn_ref[...], w_ref[...].astype(BF16))
    if residual:
        y = r_ref[...] + y

    if n_head_norm_tiles:
        @pl.when(j < n_head_norm_tiles)
        def _():
            for c in range(tn // HEAD_DIM):
                sl = slice(c * HEAD_DIM, (c + 1) * HEAD_DIM)
                o_ref[:, sl] = _rms(y[:, sl]) * cg_ref[:, sl]

        @pl.when(j >= n_head_norm_tiles)
        def _():
            o_ref[...] = y
    else:
        o_ref[...] = y


def _proj(x, w, *, norm_gain=None, norm_layer=0, head_gain=None, n_head_norm_cols=0, residual=None):
    m, k = x.shape
    n = w.shape[1]
    tm = _row_tile(m)
    tn = COL_TILE if n % COL_TILE == 0 else n
    assert n_head_norm_cols % tn == 0
    n_head_norm_tiles = n_head_norm_cols // tn
    pre_norm = norm_gain is not None
    args = [x]
    specs = [pl.BlockSpec((tm, k), lambda i, j: (i, 0))]
    if pre_norm:
        args.append(norm_gain[:, None, :])
        specs.append(pl.BlockSpec((None, 1, k), lambda i, j: (norm_layer, 0, 0)))
    args.append(w)
    specs.append(pl.BlockSpec((k, tn), lambda i, j: (0, j)))
    if n_head_norm_tiles:
        args.append(head_gain)
        specs.append(pl.BlockSpec((1, tn), lambda i, j: (0, j)))
    if residual is not None:
        args.append(residual)
        specs.append(pl.BlockSpec((tm, tn), lambda i, j: (i, j)))
    body = functools.partial(_proj_body, pre_norm=pre_norm, n_head_norm_tiles=n_head_norm_tiles,
                             residual=residual is not None, tn=tn)
    return pl.pallas_call(
        body,
        grid=(m // tm, n // tn),
        in_specs=specs,
        out_specs=pl.BlockSpec((tm, tn), lambda i, j: (i, j)),
        out_shape=jax.ShapeDtypeStruct((m, n), F32),
        scratch_shapes=[pltpu.VMEM((tm, k), BF16)],
        compiler_params=_cp("parallel", "arbitrary"),
        name="proj",
    )(*args)


def _rmsnorm_body(x_ref, g_ref, o_ref):
    o_ref[...] = _rms(x_ref[...]) * g_ref[...]


def _rmsnorm(x, gains, layer):
    m, d = x.shape
    tm = _row_tile(m)
    return pl.pallas_call(
        _rmsnorm_body,
        grid=(m // tm,),
        in_specs=[pl.BlockSpec((tm, d), lambda i: (i, 0)),
                  pl.BlockSpec((None, 1, d), lambda i: (layer, 0, 0))],
        out_specs=pl.BlockSpec((tm, d), lambda i: (i, 0)),
        out_shape=jax.ShapeDtypeStruct((m, d), F32),
        compiler_params=_cp("parallel"),
        name="rmsnorm",
    )(x, gains[:, None, :])


def _moba_prompt_body(q_ref, k_ref, v_ref, e_ref, o_ref, kmean_ref, kb_ref, vb_ref, *,
                      nblk, blk, topk, n_ranges):
    qt = pl.program_id(2)
    scale = HEAD_DIM ** -0.5

    @pl.when(qt == 0)
    def _():
        for n in range(nblk):
            kmean_ref[n:n + 1, :] = jnp.mean(k_ref[n * blk:(n + 1) * blk, :], axis=0, keepdims=True)
        kb_ref[...] = k_ref[...].astype(BF16)
        vb_ref[...] = v_ref[...].astype(BF16)

    q = q_ref[...]
    gate = _dot_hi(q, kmean_ref[...], _dot_nt)
    col = lax.broadcasted_iota(I32, gate.shape, 1)
    colf = col.astype(F32)
    gate = jnp.where(col < qt, gate, NEG_INF)
    picked = jnp.zeros(gate.shape, F32)
    for _ in range(topk):
        best = jnp.max(gate, axis=-1, keepdims=True)
        first = jnp.min(jnp.where(gate == best, colf, float(nblk)), axis=-1, keepdims=True)
        hit = colf == first
        picked = jnp.where(hit, 1.0, picked)
        gate = jnp.where(hit, -jnp.inf, gate)
    picked = jnp.where(col < qt, picked, jnp.where(col == qt, 1.0, 0.0))
    pickb = picked.astype(BF16)
    qb = (q * scale).astype(BF16)

    def attend(nk):
        kpos = lax.broadcasted_iota(I32, (blk, nk), 1)
        tpos = qt * blk + lax.broadcasted_iota(I32, (blk, nk), 0)
        in_picked = _dot(pickb, e_ref[:, 0:nk])
        s = _dot_nt(qb, kb_ref[0:nk, :])
        s = jnp.where(kpos <= tpos, jnp.where(in_picked > 0.5, s, NEG_INF), NEG_INF)
        p = jnp.exp(s - jnp.max(s, axis=-1, keepdims=True))
        l = jnp.sum(p, axis=-1, keepdims=True)
        o_ref[...] = _dot(p.astype(BF16), vb_ref[0:nk, :]) / l

    for v in range(n_ranges):
        lo, hi = v * nblk // n_ranges, (v + 1) * nblk // n_ranges

        @pl.when((qt >= lo) & (qt < hi))
        def _(hi=hi):
            attend(hi * blk)


def _moba_prompt_attn(qkv, batch, seq, heads):
    blk = MOBA_BLOCK
    nblk = seq // blk
    topk = min(MOBA_TOPK, (seq - 1) // blk)
    n_ranges = 4 if nblk % 4 == 0 else 1
    block_of_key = (jnp.arange(seq)[None, :] // blk == jnp.arange(nblk)[:, None]).astype(BF16)
    body = functools.partial(_moba_prompt_body, nblk=nblk, blk=blk, topk=topk, n_ranges=n_ranges)
    return pl.pallas_call(
        body,
        grid=(batch, heads, nblk),
        in_specs=[
            pl.BlockSpec((blk, HEAD_DIM), lambda b, h, t: (b * nblk + t, h)),
            pl.BlockSpec((seq, HEAD_DIM), lambda b, h, t: (b, heads + h)),
            pl.BlockSpec((seq, HEAD_DIM), lambda b, h, t: (b, 2 * heads + h)),
            pl.BlockSpec((nblk, seq), lambda b, h, t: (0, 0)),
        ],
        out_specs=pl.BlockSpec((blk, HEAD_DIM), lambda b, h, t: (b * nblk + t, h)),
        out_shape=jax.ShapeDtypeStruct((batch * seq, heads * HEAD_DIM), F32),
        scratch_shapes=[pltpu.VMEM((nblk, HEAD_DIM), F32), pltpu.VMEM((seq, HEAD_DIM), BF16),
                        pltpu.VMEM((seq, HEAD_DIM), BF16)],
        compiler_params=_cp("parallel", "parallel", "arbitrary"),
        name="moba_prompt",
    )(qkv, qkv, qkv, block_of_key)


def _moba_gate_body(pt_ref, k0_ref, k1_ref, q_ref, o_ref, gate_ref, *, nfull, topk):
    n = pl.program_id(1)
    lane = lax.broadcasted_iota(I32, gate_ref.shape, 1)

    @pl.when(n == 0)
    def _():
        gate_ref[...] = jnp.full(gate_ref.shape, -jnp.inf, F32)

    kmean = (jnp.sum(k0_ref[...], axis=0) + jnp.sum(k1_ref[...], axis=0)) / float(MOBA_BLOCK)
    g = jnp.sum(kmean * q_ref[...], axis=-1, keepdims=True)
    gate_ref[...] = jnp.where(lane == n, g, gate_ref[...])

    @pl.when(n == nfull - 1)
    def _():
        gate = gate_ref[...]
        lanef = lane.astype(F32)
        picks = jnp.zeros(gate.shape, F32)
        for r in range(topk):
            best = jnp.max(gate, axis=-1, keepdims=True)
            first = jnp.min(jnp.where(gate == best, lanef, float(LANES)), axis=-1, keepdims=True)
            picks = jnp.where(lane == r, first, picks)
            gate = jnp.where(lanef == first, -jnp.inf, gate)
        o_ref[...] = picks.astype(I32)


def _moba_sample_select(pool_k, q3, page_table_flat, nfull):
    bsz, heads, _ = q3.shape
    n_pages = page_table_flat.shape[0] // bsz
    topk = min(MOBA_TOPK, nfull)
    assert nfull <= LANES and MOBA_BLOCK == 2 * PAGE_SIZE
    body = functools.partial(_moba_gate_body, nfull=nfull, topk=topk)
    page = (None, PAGE_SIZE, heads, HEAD_DIM)
    grid_spec = pltpu.PrefetchScalarGridSpec(
        num_scalar_prefetch=1,
        grid=(bsz, nfull),
        in_specs=[
            pl.BlockSpec(page, lambda b, n, pt: (pt[b * n_pages + 2 * n], 0, 0, 0)),
            pl.BlockSpec(page, lambda b, n, pt: (pt[b * n_pages + 2 * n + 1], 0, 0, 0)),
            pl.BlockSpec((None, heads, HEAD_DIM), lambda b, n, pt: (b, 0, 0)),
        ],
        out_specs=pl.BlockSpec((None, heads, LANES), lambda b, n, pt: (b, 0, 0)),
        scratch_shapes=[pltpu.VMEM((heads, LANES), F32)],
    )
    return pl.pallas_call(
        body,
        grid_spec=grid_spec,
        out_shape=jax.ShapeDtypeStruct((bsz, heads, LANES), I32),
        compiler_params=_cp("parallel", "arbitrary"),
        name="moba_sample_select",
    )(page_table_flat, pool_k, pool_k, q3)


def _moba_sample_attn_body(pt_ref, sel_ref, q_ref, kn_ref, vn_ref, kp_ref, vp_ref, o_ref,
                           m_ref, l_ref, acc_ref, *, nsteps, heads):
    h = pl.program_id(1)
    r = pl.program_id(2)
    scale = HEAD_DIM ** -0.5
    q8 = jnp.broadcast_to(q_ref[...], (8, HEAD_DIM))

    @pl.when(r == 0)
    def _():
        s_own = jnp.sum(q8 * kn_ref[...], axis=-1, keepdims=True) * scale
        m_ref[...] = jnp.broadcast_to(s_own, m_ref.shape)
        l_ref[...] = jnp.ones(l_ref.shape, F32)
        acc_ref[...] = jnp.broadcast_to(vn_ref[...], acc_ref.shape)

    s = _dot_nt(q8.astype(BF16), kp_ref[...].astype(BF16)) * scale
    own = lax.broadcasted_iota(I32, s.shape, 1) % heads == h
    s = jnp.where(own, s, NEG_INF)
    m = m_ref[:, 0:1]
    m_new = jnp.maximum(m, jnp.max(s, axis=-1, keepdims=True))
    a = jnp.exp(m - m_new)
    p = jnp.exp(s - m_new)
    l_new = a * l_ref[:, 0:1] + jnp.sum(p, axis=-1, keepdims=True)
    acc_new = a * acc_ref[...] + _dot(p.astype(BF16), vp_ref[...].astype(BF16))
    m_ref[...] = jnp.broadcast_to(m_new, m_ref.shape)
    l_ref[...] = jnp.broadcast_to(l_new, l_ref.shape)
    acc_ref[...] = acc_new

    @pl.when(r == nsteps - 1)
    def _():
        o_ref[...] = (acc_new / l_new)[0:1, :]


def _moba_sample_attn(qkv3, pool_k3, pool_v3, page_table_flat, sel_flat, heads, topk):
    bsz = qkv3.shape[0]
    n_pages = page_table_flat.shape[0] // bsz
    nsteps = topk * 2

    def page(b, h, r, pt, sel):
        blk = sel[(b * heads + h) * topk + r // 2]
        return pt[b * n_pages + 2 * blk + r % 2], 0, 0

    grid_spec = pltpu.PrefetchScalarGridSpec(
        num_scalar_prefetch=2,
        grid=(bsz, heads, nsteps),
        in_specs=[
            pl.BlockSpec((None, 1, HEAD_DIM), lambda b, h, r, pt, sel: (b, 0, h)),
            pl.BlockSpec((None, 1, HEAD_DIM), lambda b, h, r, pt, sel: (b, 0, heads + h)),
            pl.BlockSpec((None, 1, HEAD_DIM), lambda b, h, r, pt, sel: (b, 0, 2 * heads + h)),
            pl.BlockSpec((None, PAGE_SIZE * heads, HEAD_DIM), page),
            pl.BlockSpec((None, PAGE_SIZE * heads, HEAD_DIM), page),
        ],
        out_specs=pl.BlockSpec((None, 1, HEAD_DIM), lambda b, h, r, pt, sel: (b, 0, h)),
        scratch_shapes=[pltpu.VMEM((8, LANES), F32), pltpu.VMEM((8, LANES), F32),
                        pltpu.VMEM((8, HEAD_DIM), F32)],
    )
    return pl.pallas_call(
        functools.partial(_moba_sample_attn_body, nsteps=nsteps, heads=heads),
        grid_spec=grid_spec,
        out_shape=jax.ShapeDtypeStruct((bsz, 1, heads * HEAD_DIM), F32),
        compiler_params=_cp("parallel", "parallel", "arbitrary"),
        name="moba_sample_attn",
    )(page_table_flat, sel_flat, qkv3, qkv3, qkv3, pool_k3, pool_v3)


def _pool_body(h_ref, prev_ref, w_ref, sc_ref, x_ref, o_ref, *, tm, tiles_per_seq, group):
    tile = pl.program_id(0) % tiles_per_seq
    h = h_ref[...]
    prev = jnp.where(tile == 0, 0.0, prev_ref[...])
    ext = jnp.concatenate([prev, h], axis=0)
    pos = tile * tm + lax.broadcasted_iota(I32, (tm, 1), 0)
    for g, w in enumerate(POOL_WINDOWS):
        sl = slice(g * group, (g + 1) * group)
        cur = ext[:, sl]
        shift = 1
        while shift < w:
            cur = cur + pltpu.roll(cur, shift, 0)
            shift *= 2
        cnt = jnp.minimum(pos + 1, w).astype(F32)
        pooled = cur[POOL_HALO:, :] / cnt - h[:, sl]
        y = _dot(pooled.astype(BF16), w_ref[g].astype(BF16))
        o_ref[:, sl] = x_ref[:, sl] + y * sc_ref[:, sl]


def _pool_mix(h, x, w_pool, scale, seq, *, tm):
    m, d = h.shape
    group = d // len(POOL_WINDOWS)
    tiles_per_seq = seq // tm
    halo_blocks = tm // POOL_HALO
    body = functools.partial(_pool_body, tm=tm, tiles_per_seq=tiles_per_seq, group=group)
    return pl.pallas_call(
        body,
        grid=(m // tm,),
        in_specs=[
            pl.BlockSpec((tm, d), lambda i: (i, 0)),
            pl.BlockSpec((POOL_HALO, d), lambda i: (jnp.maximum(i * halo_blocks - 1, 0), 0)),
            pl.BlockSpec((len(POOL_WINDOWS), group, group), lambda i: (0, 0, 0)),
            pl.BlockSpec((1, d), lambda i: (0, 0)),
            pl.BlockSpec((tm, d), lambda i: (i, 0)),
        ],
        out_specs=pl.BlockSpec((tm, d), lambda i: (i, 0)),
        out_shape=jax.ShapeDtypeStruct((m, d), F32),
        compiler_params=_cp("parallel"),
        name="pool_mix",
    )(h, h, w_pool, scale, x)


def _float_key(score):
    bits = lax.bitcast_convert_type(score, I32)
    return jnp.where(bits < 0, bits ^ 0x7FFFFFFF, bits)


def _count(mask, axes):
    ones = jnp.where(mask, 1.0, 0.0)
    for ax in (axes if isinstance(axes, tuple) else (axes,)):
        ones = jnp.sum(ones, axis=ax, keepdims=True)
    return ones


def _kth_largest_key(count_ge, k, shape):
    t = jnp.where(count_ge(jnp.zeros(shape, I32)) >= k, 0, INT_MIN).astype(I32)

    def body(i, t):
        cand = t | jnp.left_shift(jnp.int32(1), 30 - i)
        return jnp.where(count_ge(cand) >= k, cand, t)

    return lax.fori_loop(0, 31, body, t)


def _tie_limit(count_eq_below, need, shape, nbits):
    def body(i, j):
        cand = j | jnp.left_shift(jnp.int32(1), nbits - 1 - i)
        return jnp.where(count_eq_below(cand) <= need, cand, j)

    return lax.fori_loop(0, nbits, body, jnp.zeros(shape, I32))


def _dsa_prompt_body(q_ref, k_ref, v_ref, qi_ref, wq_ref, ki_ref, o_ref,
                     qs_ref, os_ref, bias_ref, lim_ref, *, seq, tq, ntop, heads, n_ranges):
    qt = pl.program_id(1)
    scale = HEAD_DIM ** -0.5
    group = heads // C_KV_HEADS
    nt = seq // tq

    qi = qi_ref[...]
    wi = wq_ref[:, IDX_DIM:IDX_DIM + IDX_HEADS] * (IDX_HEADS ** -0.5 * IDX_DIM ** -0.5)
    for h in range(heads):
        qs_ref[h] = (q_ref[:, h * HEAD_DIM:(h + 1) * HEAD_DIM] * scale).astype(BF16)

    def attend(nk):
        score = jnp.zeros((tq, nk), F32)
        for h in range(IDX_HEADS):
            d = _dot_nt(qi[:, h * IDX_DIM:(h + 1) * IDX_DIM].astype(BF16), ki_ref[0:nk, :])
            score = score + jnp.maximum(d, 0.0) * wi[:, h:h + 1]
        kpos = lax.broadcasted_iota(I32, (tq, nk), 1)
        tpos = qt * tq + lax.broadcasted_iota(I32, (tq, nk), 0)
        causal = kpos <= tpos
        key = _float_key(jnp.where(causal, score, NEG_INF))

        thr = _kth_largest_key(lambda c: _count(key >= c, -1), float(ntop), (tq, 1))
        above = _count(key > thr, -1)
        tied = key == thr
        need = float(ntop) - above
        lim_ref[...] = jnp.full((tq, 1), nk, I32)

        @pl.when(jnp.max(_count(tied, -1) - need) > 0.0)
        def _():
            lim_ref[...] = _tie_limit(lambda j: _count(tied & (kpos < j), -1), need, (tq, 1),
                                      nk.bit_length())

        chosen = (key > thr) | (tied & (kpos < lim_ref[...]))
        bias_ref[:, 0:nk] = jnp.where(chosen & causal, 0.0, NEG_INF)

        for g in range(C_KV_HEADS):
            sl = slice(g * HEAD_DIM, (g + 1) * HEAD_DIM)

            def head(j, carry, g=g, sl=sl):
                h = g * group + j
                s = _dot_nt(qs_ref[h], k_ref[0:nk, sl]) + bias_ref[:, 0:nk]
                p = jnp.exp(s - jnp.max(s, axis=-1, keepdims=True))
                l = jnp.sum(p, axis=-1, keepdims=True)
                os_ref[h] = _dot(p.astype(BF16), v_ref[0:nk, sl]) / l
                return carry

            lax.fori_loop(0, group, head, 0)

    for v in range(n_ranges):
        lo, hi = v * nt // n_ranges, (v + 1) * nt // n_ranges

        @pl.when((qt >= lo) & (qt < hi))
        def _(hi=hi):
            attend(hi * tq)

    for h in range(heads):
        o_ref[:, h * HEAD_DIM:(h + 1) * HEAD_DIM] = os_ref[h]


def _dsa_prompt_attn(proj, tail, k_bf, v_bf, ki_bf, batch, seq, heads):
    tq = DSA_Q_TILE
    nt = seq // tq
    ntop = min(DSA_TOPK, seq // 4)
    qw = heads * HEAD_DIM
    kvw = C_KV_HEADS * HEAD_DIM
    iw = IDX_HEADS * IDX_DIM
    assert (qw + 2 * kvw) % iw == 0
    n_ranges = 4 if nt % 4 == 0 else 1
    body = functools.partial(_dsa_prompt_body, seq=seq, tq=tq, ntop=ntop, heads=heads, n_ranges=n_ranges)
    return pl.pallas_call(
        body,
        grid=(batch, nt),
        in_specs=[
            pl.BlockSpec((tq, qw), lambda b, t: (b * nt + t, 0)),
            pl.BlockSpec((seq, kvw), lambda b, t: (b, 0)),
            pl.BlockSpec((seq, kvw), lambda b, t: (b, 0)),
            pl.BlockSpec((tq, iw), lambda b, t: (b * nt + t, (qw + 2 * kvw) // iw)),
            pl.BlockSpec((tq, tail.shape[1]), lambda b, t: (b * nt + t, 0)),
            pl.BlockSpec((seq, IDX_DIM), lambda b, t: (b, 0)),
        ],
        out_specs=pl.BlockSpec((tq, qw), lambda b, t: (b * nt + t, 0)),
        out_shape=jax.ShapeDtypeStruct((batch * seq, qw), F32),
        scratch_shapes=[pltpu.VMEM((heads, tq, HEAD_DIM), BF16), pltpu.VMEM((heads, tq, HEAD_DIM), F32),
                        pltpu.VMEM((tq, seq), F32), pltpu.VMEM((tq, 1), I32)],
        compiler_params=_cp("parallel", "arbitrary"),
        name="dsa_prompt",
    )(proj, k_bf, v_bf, proj, tail, ki_bf)


def _index_score_rows(qi, wi, ki):
    d = _dot_nt(qi.astype(BF16), ki.astype(BF16)) * (IDX_DIM ** -0.5)
    return jnp.sum(jnp.maximum(d, 0.0) * (wi * (IDX_HEADS ** -0.5)), axis=0, keepdims=True)


def _dsa_sample_scores_body(pt_ref, *refs, bsz):
    ki_refs = refs[:bsz]
    qi_ref, wi_ref, o_ref = refs[bsz:]
    p = pl.program_id(0)
    for b in range(bsz):
        o_ref[b, pl.ds(p, 1), :] = _index_score_rows(qi_ref[b], wi_ref[b], ki_refs[b][...])


def _dsa_sample_scores(pool_ki, qi3, wi3, page_table_flat):
    bsz = qi3.shape[0]
    n_pages = page_table_flat.shape[0] // bsz
    page_specs = [
        pl.BlockSpec((None, PAGE_SIZE, IDX_DIM), lambda p, pt, b=b: (pt[b * n_pages + p], 0, 0))
        for b in range(bsz)
    ]
    grid_spec = pltpu.PrefetchScalarGridSpec(
        num_scalar_prefetch=1,
        grid=(n_pages,),
        in_specs=page_specs + [
            pl.BlockSpec(qi3.shape, lambda p, pt: (0, 0, 0)),
            pl.BlockSpec(wi3.shape, lambda p, pt: (0, 0, 0)),
        ],
        out_specs=pl.BlockSpec((bsz, n_pages, PAGE_SIZE), lambda p, pt: (0, 0, 0)),
    )
    return pl.pallas_call(
        functools.partial(_dsa_sample_scores_body, bsz=bsz),
        grid_spec=grid_spec,
        out_shape=jax.ShapeDtypeStruct((bsz, n_pages, PAGE_SIZE), F32),
        compiler_params=_cp("arbitrary"),
        name="dsa_sample_scores",
    )(page_table_flat, *([pool_ki] * bsz), qi3, wi3)


def _dsa_sample_attn_body(pt_ref, sc_ref, qi_ref, wi_ref, kin_ref, q_ref, kn_ref, vn_ref, *refs,
                          n_pages, ntop, heads, pages_per_step):
    kp_refs = refs[:pages_per_step]
    vp_refs = refs[pages_per_step:2 * pages_per_step]
    o_ref, mask_ref, new_ref, m_ref, l_ref, acc_ref = refs[2 * pages_per_step:]
    p = pl.program_id(1)
    scale = HEAD_DIM ** -0.5
    group = heads // C_KV_HEADS
    past = n_pages * PAGE_SIZE

    @pl.when(p == 0)
    def _():
        key = _float_key(sc_ref[...])
        d_new = jnp.sum(qi_ref[...] * kin_ref[...], axis=-1, keepdims=True) * (IDX_DIM ** -0.5)
        key_new = _float_key(jnp.sum(jnp.maximum(d_new, 0.0) * (wi_ref[...] * (IDX_HEADS ** -0.5)),
                                     axis=0, keepdims=True))
        pos = (lax.broadcasted_iota(I32, key.shape, 0) * PAGE_SIZE
               + lax.broadcasted_iota(I32, key.shape, 1))

        def count_ge(c):
            return _count(key >= c, (0, 1)) + jnp.where(key_new >= c, 1.0, 0.0)

        thr = _kth_largest_key(count_ge, float(ntop), (1, 1))
        above = _count(key > thr, (0, 1)) + jnp.where(key_new > thr, 1.0, 0.0)
        tied = key == thr
        tied_new = key_new == thr
        need = float(ntop) - above

        def count_eq_below(j):
            return _count(tied & (pos < j), (0, 1)) + jnp.where(tied_new & (past < j), 1.0, 0.0)

        lim = _tie_limit(count_eq_below, need, (1, 1), (past + 1).bit_length())
        mask_ref[...] = jnp.where((key > thr) | (tied & (pos < lim)), 1.0, 0.0)
        new_ref[...] = jnp.broadcast_to(
            jnp.where((key_new > thr) | (tied_new & (past < lim)), 1.0, 0.0), new_ref.shape)
        m_ref[...] = jnp.full(m_ref.shape, NEG_INF, F32)
        l_ref[...] = jnp.zeros(l_ref.shape, F32)
        acc_ref[...] = jnp.zeros(acc_ref.shape, F32)

    q = q_ref[...]
    qb = q.astype(BF16)
    kv_head = lax.broadcasted_iota(I32, (heads, 1), 0) // group

    def per_kv_head(f):
        out = f(0)
        for g in range(1, C_KV_HEADS):
            out = jnp.where(kv_head == g, f(g), out)
        return out

    def page_scores(j):
        s_j = per_kv_head(lambda g: _dot_nt(qb, kp_refs[j][:, g, :].astype(BF16))) * scale
        return jnp.where(mask_ref[pl.ds(p * pages_per_step + j, 1), :] > 0.0, s_j, NEG_INF)

    s = jnp.concatenate([page_scores(j) for j in range(pages_per_step)], axis=1)
    m = m_ref[...]
    m_new = jnp.maximum(m, jnp.max(s, axis=-1, keepdims=True))
    a = jnp.exp(m - m_new)
    pr = jnp.where(s > NEG_INF, jnp.exp(s - m_new), 0.0)
    prb = pr.astype(BF16)
    pv = jnp.zeros(acc_ref.shape, F32)
    for j in range(pages_per_step):
        pr_j = prb[:, j * PAGE_SIZE:(j + 1) * PAGE_SIZE]
        pv = pv + per_kv_head(lambda g: _dot(pr_j, vp_refs[j][:, g, :].astype(BF16)))
    l_new = a * l_ref[...] + jnp.sum(pr, axis=-1, keepdims=True)
    acc_new = a * acc_ref[...] + pv
    m_ref[...] = m_new
    l_ref[...] = l_new
    acc_ref[...] = acc_new

    @pl.when(p == n_pages // pages_per_step - 1)
    def _():
        s_new = jnp.sum(q * kn_ref[...], axis=-1, keepdims=True) * scale
        take = new_ref[:, 0:1] > 0.0
        s_new = jnp.where(take, s_new, NEG_INF)
        m_fin = jnp.maximum(m_new, s_new)
        a2 = jnp.exp(m_new - m_fin)
        p_new = jnp.where(take, jnp.exp(s_new - m_fin), 0.0)
        l_fin = a2 * l_new + p_new
        o_ref[...] = (a2 * acc_new + p_new * vn_ref[...]) / l_fin


def _dsa_sample_attn(scores, qi3, wi3, kin3, q3, kn3, vn3, pool_k, pool_v, page_table_flat, heads):
    bsz, n_pages, _ = scores.shape
    ntop = min(DSA_TOPK, (n_pages * PAGE_SIZE + 1) // 4)
    pps = 4 if n_pages % 4 == 0 else 1
    per_b = lambda shape: pl.BlockSpec((None,) + shape, lambda b, p, pt: (b, 0, 0))
    pages = [pl.BlockSpec((None, PAGE_SIZE, C_KV_HEADS, HEAD_DIM),
                          lambda b, p, pt, j=j: (pt[b * n_pages + p * pps + j], 0, 0, 0))
             for j in range(pps)]
    grid_spec = pltpu.PrefetchScalarGridSpec(
        num_scalar_prefetch=1,
        grid=(bsz, n_pages // pps),
        in_specs=[per_b((n_pages, PAGE_SIZE)), per_b((IDX_HEADS, IDX_DIM)), per_b((IDX_HEADS, 1)),
                  per_b((1, IDX_DIM)), per_b((heads, HEAD_DIM)), per_b((heads, HEAD_DIM)),
                  per_b((heads, HEAD_DIM))] + pages + pages,
        out_specs=per_b((heads, HEAD_DIM)),
        scratch_shapes=[pltpu.VMEM((n_pages, PAGE_SIZE), F32), pltpu.VMEM((heads, LANES), F32),
                        pltpu.VMEM((heads, 1), F32), pltpu.VMEM((heads, 1), F32),
                        pltpu.VMEM((heads, HEAD_DIM), F32)],
    )
    body = functools.partial(_dsa_sample_attn_body, n_pages=n_pages, ntop=ntop, heads=heads,
                             pages_per_step=pps)
    return pl.pallas_call(
        body,
        grid_spec=grid_spec,
        out_shape=jax.ShapeDtypeStruct((bsz, heads, HEAD_DIM), F32),
        compiler_params=_cp("parallel", "arbitrary"),
        name="dsa_sample_attn",
    )(page_table_flat, scores, qi3, wi3, kin3, q3, kn3, vn3, *([pool_k] * pps), *([pool_v] * pps))


def _gla_body(q_ref, k_ref, v_ref, r_ref, low_ref, wg2_ref, bg_ref, gn_ref, s0_ref, o_ref, sf_ref,
              state_ref, *, chunk, n_valid, dk, dv):
    c = pl.program_id(2)

    @pl.when(c == 0)
    def _():
        state_ref[...] = s0_ref[...]

    x = _dot_hi(low_ref[...], wg2_ref[...]) + bg_ref[...]
    g = (jnp.minimum(x, 0.0) - jnp.log(1.0 + jnp.exp(-jnp.abs(x)))) / GLA_TAU
    row = lax.broadcasted_iota(I32, (chunk, 1), 0)
    if n_valid < chunk:
        g = jnp.where(row < n_valid, g, 0.0)
    tri = (lax.broadcasted_iota(I32, (chunk, chunk), 1)
           <= lax.broadcasted_iota(I32, (chunk, chunk), 0)).astype(BF16)
    bcum = _dot_exact_lhs(tri, g)
    q = q_ref[...] * (dk ** -0.5)
    k = k_ref[...]
    vb = v_ref[...].astype(BF16)
    sub = 8
    lane = lax.broadcasted_iota(I32, (sub, chunk), 1)
    row_groups = [jnp.zeros((sub, chunk), F32) for _ in range(chunk // sub)]
    for s in range(chunk):
        r0 = (s // sub) * sub
        rows = r0 + lax.broadcasted_iota(I32, (chunk - r0, 1), 0)
        decay = jnp.exp(jnp.where(rows >= s, bcum[r0:, :] - bcum[s:s + 1, :], -jnp.inf))
        col = jnp.sum(q[r0:, :] * k[s:s + 1, :] * decay, axis=-1, keepdims=True)
        for rg in range(s // sub, chunk // sub):
            part = col[rg * sub - r0:(rg + 1) * sub - r0, :]
            row_groups[rg] = jnp.where(lane == s, part, row_groups[rg])
    att = jnp.concatenate(row_groups, axis=0)
    state = state_ref[...]
    o = _dot(att.astype(BF16), vb) + _dot((q * jnp.exp(bcum)).astype(BF16), state.astype(BF16))

    b_last = bcum[chunk - 1:chunk, :]
    last_row = (lax.broadcasted_iota(I32, (chunk, dv), 0) == chunk - 1).astype(BF16)
    b_last_cols = _dot_exact_rhs(bcum, last_row, _dot_tn)
    kd = (k * jnp.exp(b_last - bcum)).astype(BF16)
    new_state = jnp.exp(b_last_cols) * state + _dot_tn(kd, vb)
    state_ref[...] = new_state

    @pl.when(c == pl.num_programs(2) - 1)
    def _():
        sf_ref[...] = new_state

    o_ref[...] = _rms(o) * gn_ref[...] * _silu(r_ref[...])


def _gla(proj3, low3, w_g2, b_g, g_n, s0, *, chunk, n_valid):
    bsz, length, _ = proj3.shape
    heads = GLA_HEADS
    dk, dv = s0.shape[2], s0.shape[3]
    assert dv == 2 * dk
    body = functools.partial(_gla_body, chunk=chunk, n_valid=n_valid, dk=dk, dv=dv)
    return pl.pallas_call(
        body,
        grid=(bsz, heads, length // chunk),
        in_specs=[
            pl.BlockSpec((None, chunk, dk), lambda b, h, c: (b, c, h)),
            pl.BlockSpec((None, chunk, dk), lambda b, h, c: (b, c, heads + h)),
            pl.BlockSpec((None, chunk, dv), lambda b, h, c: (b, c, heads + h)),
            pl.BlockSpec((None, chunk, dv), lambda b, h, c: (b, c, 2 * heads + h)),
            pl.BlockSpec((None, chunk, low3.shape[2]), lambda b, h, c: (b, c, 0)),
            pl.BlockSpec((low3.shape[2], dk), lambda b, h, c: (0, h)),
            pl.BlockSpec((1, dk), lambda b, h, c: (0, h)),
            pl.BlockSpec((1, dv), lambda b, h, c: (0, 0)),
            pl.BlockSpec((None, None, dk, dv), lambda b, h, c: (b, h, 0, 0)),
        ],
        out_specs=[
            pl.BlockSpec((None, chunk, dv), lambda b, h, c: (b, c, h)),
            pl.BlockSpec((None, None, dk, dv), lambda b, h, c: (b, h, 0, 0)),
        ],
        out_shape=[jax.ShapeDtypeStruct((bsz, length, heads * dv), F32),
                   jax.ShapeDtypeStruct(s0.shape, F32)],
        scratch_shapes=[pltpu.VMEM((dk, dv), F32)],
        compiler_params=_cp("parallel", "parallel", "arbitrary"),
        name="gla",
    )(proj3, proj3, proj3, proj3, low3, w_g2, b_g, g_n, s0)


def _mixer_a(xp, xs, batch, seq, norm_mix, layer, cache_k, cache_v, page_table, w_qkv, w_o, g_q, g_k):
    d = xp.shape[1]
    heads = d // HEAD_DIM
    gain = jnp.concatenate([jnp.tile(g_q, heads), jnp.tile(g_k, heads), jnp.ones((d,), F32)])[None, :]

    def qkv(x):
        return _proj(x, w_qkv, norm_gain=norm_mix, norm_layer=layer, head_gain=gain, n_head_norm_cols=2 * d)

    qkv_p = qkv(xp)
    o_p = _moba_prompt_attn(qkv_p, batch, seq, heads)
    xp = _proj(o_p, w_o, residual=xp)

    bsz = xs.shape[0]
    n_pages = page_table.shape[1]
    past = n_pages * PAGE_SIZE
    assert past % MOBA_BLOCK == 0
    nfull = past // MOBA_BLOCK
    topk = min(MOBA_TOPK, nfull)
    assert topk > 0
    qkv_s = qkv(xs)
    qkv_s3 = qkv_s[:, None, :]
    pool_k3 = cache_k.reshape(cache_k.shape[0], PAGE_SIZE * heads, HEAD_DIM)
    pool_v3 = cache_v.reshape(cache_v.shape[0], PAGE_SIZE * heads, HEAD_DIM)
    pt_flat = page_table.reshape(-1)
    sel = _moba_sample_select(cache_k, qkv_s[:, :d].reshape(bsz, heads, HEAD_DIM), pt_flat, nfull)
    sel_flat = sel[:, :, :topk].reshape(-1)
    o_s = _moba_sample_attn(qkv_s3, pool_k3, pool_v3, pt_flat, sel_flat, heads, topk)
    xs = _proj(o_s[:, 0, :], w_o, residual=xs)

    kv_shape = lambda t, n: t.reshape(n, -1, heads, HEAD_DIM)
    outs = (kv_shape(qkv_p[:, d:2 * d], batch), kv_shape(qkv_p[:, 2 * d:], batch),
            kv_shape(qkv_s[:, d:2 * d], bsz), kv_shape(qkv_s[:, 2 * d:], bsz))
    return xp, xs, outs


def _mixer_b(xp, xs, batch, seq, norm_mix, layer, state_pool, w_pool, scale):
    d = xp.shape[1]
    keep = state_pool.shape[1]
    scale = scale[None, :]
    hp = _rmsnorm(xp, norm_mix, layer)
    xp = _pool_mix(hp, xp, w_pool, scale, seq, tm=512)
    pool_p = hp.reshape(batch, seq, d)[:, seq - keep:]

    bsz = xs.shape[0]
    hs = _rmsnorm(xs, norm_mix, layer)
    ext = jnp.concatenate([state_pool, hs[:, None, :]], axis=1)
    assert keep + 1 == POOL_HALO
    x_ext = jnp.concatenate([jnp.zeros_like(state_pool), xs[:, None, :]], axis=1)
    y = _pool_mix(ext.reshape(bsz * POOL_HALO, d), x_ext.reshape(bsz * POOL_HALO, d), w_pool, scale,
                  POOL_HALO, tm=POOL_HALO)
    xs = y.reshape(bsz, POOL_HALO, d)[:, -1]
    return xp, xs, (pool_p, ext[:, 1:])


def _mixer_c(xp, xs, batch, seq, norm_mix, layer, cache_k, cache_v, cache_ki, page_table,
             w_in, w_o, g_q, g_k):
    d = xp.shape[1]
    heads = d // HEAD_DIM
    kvw = C_KV_HEADS * HEAD_DIM
    iw = IDX_HEADS * IDX_DIM
    main = d + 2 * kvw + iw
    w_main, w_tail = w_in[:, :main], w_in[:, main:]
    gain = jnp.concatenate([jnp.tile(g_q, heads), jnp.tile(g_k, C_KV_HEADS),
                            jnp.ones((main - d - kvw,), F32)])[None, :]

    def project(x):
        pm = _proj(x, w_main, norm_gain=norm_mix, norm_layer=layer, head_gain=gain, n_head_norm_cols=d + kvw)
        pt = _proj(x, w_tail, norm_gain=norm_mix, norm_layer=layer)
        return pm, pt

    pm, pt = project(xp)
    k_p, v_p, ki_p = pm[:, d:d + kvw], pm[:, d + kvw:d + 2 * kvw], pt[:, :IDX_DIM]
    o_p = _dsa_prompt_attn(pm, pt, k_p.astype(BF16), v_p.astype(BF16), ki_p.astype(BF16), batch, seq, heads)
    xp = _proj(o_p, w_o, residual=xp)

    bsz = xs.shape[0]
    group = heads // C_KV_HEADS
    sm, st = project(xs)
    k_s, v_s, ki_s = sm[:, d:d + kvw], sm[:, d + kvw:d + 2 * kvw], st[:, :IDX_DIM]
    qi3 = sm[:, d + 2 * kvw:].reshape(bsz, IDX_HEADS, IDX_DIM)
    wi3 = st[:, IDX_DIM:IDX_DIM + IDX_HEADS].reshape(bsz, IDX_HEADS, 1)
    pt_flat = page_table.reshape(-1)
    scores = _dsa_sample_scores(cache_ki, qi3, wi3, pt_flat)
    per_q_head = lambda t: jnp.repeat(t.reshape(bsz, C_KV_HEADS, HEAD_DIM), group, axis=1)
    o_s = _dsa_sample_attn(scores, qi3, wi3, ki_s[:, None, :], sm[:, :d].reshape(bsz, heads, HEAD_DIM),
                           per_q_head(k_s), per_q_head(v_s), cache_k, cache_v, pt_flat, heads)
    xs = _proj(o_s.reshape(bsz, d), w_o, residual=xs)

    kv4 = lambda t, n: t.reshape(n, -1, C_KV_HEADS, HEAD_DIM)
    outs = (kv4(k_p, batch), kv4(v_p, batch), ki_p.reshape(batch, seq, IDX_DIM),
            kv4(k_s, bsz), kv4(v_s, bsz), ki_s.reshape(bsz, 1, IDX_DIM))
    return xp, xs, outs


def _mixer_d(xp, xs, batch, seq, norm_mix, layer, state, w_in, w_g2, b_g, g_n, w_o):
    d = xp.shape[1]
    dk, dv = state.shape[2], state.shape[3]
    main = 2 * GLA_HEADS * dk + 2 * GLA_HEADS * dv
    w_main, w_tail = w_in[:, :main], w_in[:, main:]

    def project(x):
        pm = _proj(x, w_main, norm_gain=norm_mix, norm_layer=layer)
        pt = _proj(x, w_tail, norm_gain=norm_mix, norm_layer=layer)
        return pm, pt

    pm, pt = project(xp)
    zero_state = jnp.zeros((batch,) + state.shape[1:], F32)
    o_p, s_p = _gla(pm.reshape(batch, seq, main), pt.reshape(batch, seq, -1), w_g2, b_g[None, :],
                    g_n[None, :], zero_state, chunk=min(GLA_CHUNK, seq), n_valid=min(GLA_CHUNK, seq))
    xp = _proj(o_p.reshape(batch * seq, GLA_HEADS * dv), w_o, residual=xp)

    bsz = xs.shape[0]
    sm, st = project(xs)
    pad = lambda t: jnp.pad(t[:, None, :], ((0, 0), (0, GLA_SAMPLE_ROWS - 1), (0, 0)))
    o_s, s_s = _gla(pad(sm), pad(st), w_g2, b_g[None, :], g_n[None, :], state,
                    chunk=GLA_SAMPLE_ROWS, n_valid=1)
    xs = _proj(o_s[:, 0, :], w_o, residual=xs)
    return xp, xs, (s_p, s_s)


def kernel(x_prompt, x_sample, cache_a_k, cache_a_v, state_b_pool, cache_c_k, cache_c_v, cache_c_idx_k,
           state_d_gla, page_table, norm_ffn1, ffn1_w_gate, ffn1_w_up, ffn1_w_down, norm_mix, norm_ffn2,
           ffn2_w_gate, ffn2_w_up, ffn2_w_down, a_w_qkv, a_w_o, a_g_q, a_g_k, b_w_pool, b_scale, c_w_in,
           c_w_o, c_g_q, c_g_k, d_w_in, d_w_g2, d_b_g, d_g_n, d_w_o):
    batch, seq, d = x_prompt.shape
    bsz = x_sample.shape[0]
    assert x_sample.shape[1] == 1
    xp = x_prompt.reshape(batch * seq, d)
    xs = x_sample.reshape(bsz, d)
    depth = norm_ffn1.shape[0]
    outs = {}
    for i in range(depth):
        xp = _ffn(xp, norm_ffn1, ffn1_w_gate, ffn1_w_up, ffn1_w_down, i)
        xs = _ffn(xs, norm_ffn1, ffn1_w_gate, ffn1_w_up, ffn1_w_down, i)
        m = i % 4
        if m == 0:
            xp, xs, outs["a"] = _mixer_a(xp, xs, batch, seq, norm_mix, i, cache_a_k, cache_a_v, page_table,
                                         a_w_qkv, a_w_o, a_g_q, a_g_k)
        elif m == 1:
            xp, xs, outs["b"] = _mixer_b(xp, xs, batch, seq, norm_mix, i, state_b_pool, b_w_pool, b_scale)
        elif m == 2:
            xp, xs, outs["c"] = _mixer_c(xp, xs, batch, seq, norm_mix, i, cache_c_k, cache_c_v, cache_c_idx_k,
                                         page_table, c_w_in, c_w_o, c_g_q, c_g_k)
        else:
            xp, xs, outs["d"] = _mixer_d(xp, xs, batch, seq, norm_mix, i, state_d_gla, d_w_in, d_w_g2,
                                         d_b_g, d_g_n, d_w_o)
        xp = _ffn(xp, norm_ffn2, ffn2_w_gate, ffn2_w_up, ffn2_w_down, i)
        xs = _ffn(xs, norm_ffn2, ffn2_w_gate, ffn2_w_up, ffn2_w_down, i)
    return (xp.reshape(batch, seq, d), xs.reshape(bsz, 1, d)) + outs["a"] + outs["b"] + outs["c"] + outs["d"]
```

```python
import functools

import jax
import jax.numpy as jnp
from jax import lax
from jax.experimental import pallas as pl
from jax.experimental.pallas import tpu as pltpu

F32 = jnp.float32
BF16 = jnp.bfloat16
I32 = jnp.int32

NORM_EPS = 1e-6
NEG_INF = -1e30
HEAD_DIM = 128
LANES = 128
PAGE_SIZE = 128
MOBA_BLOCK = 256
MOBA_TOPK = 3
POOL_WINDOWS = (2, 4, 8, 16)
POOL_HALO = 16
C_KV_HEADS = 4
IDX_HEADS = 16
IDX_DIM = 64
DSA_TOPK = 256
DSA_Q_TILE = 128
GLA_HEADS = 4
GLA_GATE_RANK = 16
GLA_TAU = 16.0
GLA_CHUNK = 64
GLA_SAMPLE_ROWS = 16
VMEM_LIMIT = 56 * 1024 * 1024

INT_MIN = -2147483648


def _cp(*sem):
    return pltpu.CompilerParams(dimension_semantics=sem, vmem_limit_bytes=VMEM_LIMIT)


ROW_TILE = 1024
COL_TILE = 512
FFN_OUT_TILE = 256


def _row_tile(m):
    return ROW_TILE if m % ROW_TILE == 0 else m


FFN_HIDDEN_TILE = 512


def _dot(a, b):
    return jnp.dot(a, b, preferred_element_type=F32)


def _dot_nt(a, b):
    return lax.dot_general(a, b, (((1,), (1,)), ((), ())), preferred_element_type=F32)


def _dot_tn(a, b):
    return lax.dot_general(a, b, (((0,), (0,)), ((), ())), preferred_element_type=F32)


def _split3(a):
    a1 = a.astype(BF16)
    r1 = a - a1.astype(F32)
    a2 = r1.astype(BF16)
    a3 = (r1 - a2.astype(F32)).astype(BF16)
    return a1, a2, a3


def _dot_hi(a, b, dot=_dot):
    a1, a2, _ = _split3(a)
    b1, b2, _ = _split3(b)
    return dot(a1, b1) + (dot(a1, b2) + dot(a2, b1))


def _dot_exact_lhs(a01, b, dot=_dot):
    b1, b2, b3 = _split3(b)
    return dot(a01, b1) + (dot(a01, b2) + dot(a01, b3))


def _dot_exact_rhs(a, b01, dot=_dot):
    a1, a2, a3 = _split3(a)
    return dot(a1, b01) + (dot(a2, b01) + dot(a3, b01))


def _rms(x):
    return x * lax.rsqrt(jnp.mean(x * x, axis=-1, keepdims=True) + NORM_EPS)


def _silu(x):
    return x * jax.nn.sigmoid(x)


def _ffn_body(x_ref, g_ref, wg_ref, wu_ref, wd_ref, res_ref, o_ref, xn_ref, h_ref, *, n_up, tf):
    j = pl.program_id(1)

    @pl.when(j == 0)
    def _():
        xn_ref[...] = (_rms(x_ref[...]) * g_ref[...]).astype(BF16)

    @pl.when(j < n_up)
    def _():
        xn = xn_ref[...]
        a = _dot(xn, wg_ref[...])
        u = _dot(xn, wu_ref[...])
        hidden = (_silu(a) * u).astype(BF16)
        for c in range(n_up):
            @pl.when(j == c)
            def _(c=c):
                h_ref[:, c * tf:(c + 1) * tf] = hidden

    @pl.when(j >= n_up)
    def _():
        o_ref[...] = res_ref[...] + 0.5 * _dot(h_ref[...], wd_ref[...])


def _ffn(x, g, wg, wu, wd, layer):
    m, d = x.shape
    f = wg.shape[-1]
    tm = _row_tile(m)
    tf = FFN_HIDDEN_TILE
    tn = FFN_OUT_TILE
    n_up = f // tf
    up = lambda j: jnp.minimum(j, n_up - 1)
    down = lambda j: jnp.maximum(j - n_up, 0)
    return pl.pallas_call(
        functools.partial(_ffn_body, n_up=n_up, tf=tf),
        grid=(m // tm, n_up + d // tn),
        in_specs=[
            pl.BlockSpec((tm, d), lambda i, j: (i, 0), pipeline_mode=pl.Buffered(1)),
            pl.BlockSpec((None, 1, d), lambda i, j: (layer, 0, 0)),
            pl.BlockSpec((None, d, tf), lambda i, j: (layer, 0, up(j))),
            pl.BlockSpec((None, d, tf), lambda i, j: (layer, 0, up(j))),
            pl.BlockSpec((None, f, tn), lambda i, j: (layer, 0, down(j))),
            pl.BlockSpec((tm, tn), lambda i, j: (i, down(j))),
        ],
        out_specs=pl.BlockSpec((tm, tn), lambda i, j: (i, down(j))),
        out_shape=jax.ShapeDtypeStruct((m, d), F32),
        scratch_shapes=[pltpu.VMEM((tm, d), BF16), pltpu.VMEM((tm, f), BF16)],
        compiler_params=_cp("parallel", "arbitrary"),
        name="ffn",
    )(x, g[:, None, :], wg, wu, wd, x)


def _proj_body(*refs, pre_norm, n_head_norm_tiles, residual, tn):
    it = iter(refs)
    x_ref = next(it)
    g_ref = next(it) if pre_norm else None
    w_ref = next(it)
    cg_ref = next(it) if n_head_norm_tiles else None
    r_ref = next(it) if residual else None
    o_ref = next(it)
    xn_ref = next(it)
    j = pl.program_id(1)

    @pl.when(j == 0)
    def _():
        x = x_ref[...]
        if pre_norm:
            x = _rms(x) * g_ref[...]
        xn_ref[...] = x.astype(BF16)

    y = _dot(xn_ref[...], w_ref[...].astype(BF16))
    if residual:
        y = r_ref[...] + y

    if n_head_norm_tiles:
        @pl.when(j < n_head_norm_tiles)
        def _():
            for c in range(tn // HEAD_DIM):
                sl = slice(c * HEAD_DIM, (c + 1) * HEAD_DIM)
                o_ref[:, sl] = _rms(y[:, sl]) * cg_ref[:, sl]

        @pl.when(j >= n_head_norm_tiles)
        def _():
            o_ref[...] = y
    else:
        o_ref[...] = y


def _proj(x, w, *, norm_gain=None, norm_layer=0, head_gain=None, n_head_norm_cols=0, residual=None):
    m, k = x.shape
    n = w.shape[1]
    tm = _row_tile(m)
    tn = COL_TILE if n % COL_TILE == 0 else n
    assert n_head_norm_cols % tn == 0
    n_head_norm_tiles = n_head_norm_cols // tn
    pre_norm = norm_gain is not None
    args = [x]
    specs = [pl.BlockSpec((tm, k), lambda i, j: (i, 0))]
    if pre_norm:
        args.append(norm_gain[:, None, :])
        specs.append(pl.BlockSpec((None, 1, k), lambda i, j: (norm_layer, 0, 0)))
    args.append(w)
    specs.append(pl.BlockSpec((k, tn), lambda i, j: (0, j)))
    if n_head_norm_tiles:
        args.append(head_gain)
        specs.append(pl.BlockSpec((1, tn), lambda i, j: (0, j)))
    if residual is not None:
        args.append(residual)
        specs.append(pl.BlockSpec((tm, tn), lambda i, j: (i, j)))
    body = functools.partial(_proj_body, pre_norm=pre_norm, n_head_norm_tiles=n_head_norm_tiles,
                             residual=residual is not None, tn=tn)
    return pl.pallas_call(
        body,
        grid=(m // tm, n // tn),
        in_specs=specs,
        out_specs=pl.BlockSpec((tm, tn), lambda i, j: (i, j)),
        out_shape=jax.ShapeDtypeStruct((m, n), F32),
        scratch_shapes=[pltpu.VMEM((tm, k), BF16)],
        compiler_params=_cp("parallel", "arbitrary"),
        name="proj",
    )(*args)


def _rmsnorm_body(x_ref, g_ref, o_ref):
    o_ref[...] = _rms(x_ref[...]) * g_ref[...]


def _rmsnorm(x, gains, layer):
    m, d = x.shape
    tm = _row_tile(m)
    return pl.pallas_call(
        _rmsnorm_body,
        grid=(m // tm,),
        in_specs=[pl.BlockSpec((tm, d), lambda i: (i, 0)),
                  pl.BlockSpec((None, 1, d), lambda i: (layer, 0, 0))],
        out_specs=pl.BlockSpec((tm, d), lambda i: (i, 0)),
        out_shape=jax.ShapeDtypeStruct((m, d), F32),
        compiler_params=_cp("parallel"),
        name="rmsnorm",
    )(x, gains[:, None, :])


def _moba_prompt_body(q_ref, k_ref, v_ref, o_ref, kmean_ref, qa_ref, ka_ref, vb_ref, *,
                      seq, nblk, blk, topk, n_ranges):
    qt = pl.program_id(2)
    scale = HEAD_DIM ** -0.5
    big = -NEG_INF

    @pl.when(qt == 0)
    def _():
        kmean_ref[...] = jnp.zeros(kmean_ref.shape, F32)
        for n in range(nblk):
            kmean_ref[n:n + 1, :] = jnp.mean(k_ref[n * blk:(n + 1) * blk, :], axis=0, keepdims=True)
        q = q_ref[...]
        gate = _dot_hi(q, kmean_ref[...], _dot_nt)
        col = lax.broadcasted_iota(I32, gate.shape, 1)
        colf = col.astype(F32)
        own = lax.broadcasted_iota(I32, gate.shape, 0) // blk
        gate = jnp.where(col < own, gate, NEG_INF)
        picked = jnp.zeros(gate.shape, F32)
        for _ in range(topk):
            best = jnp.max(gate, axis=-1, keepdims=True)
            first = jnp.min(jnp.where(gate == best, colf, float(LANES)), axis=-1, keepdims=True)
            hit = colf == first
            picked = jnp.where(hit, 1.0, picked)
            gate = jnp.where(hit, -jnp.inf, gate)
        picked = jnp.where(col < own, picked, jnp.where(col == own, 1.0, 0.0))
        qa_ref[:, 0:HEAD_DIM] = (q * scale).astype(BF16)
        qa_ref[:, HEAD_DIM:] = (picked - 1.0).astype(BF16)
        key_block = lax.broadcasted_iota(I32, (seq, LANES), 0) // blk
        ka_ref[:, 0:HEAD_DIM] = k_ref[...].astype(BF16)
        ka_ref[:, HEAD_DIM:] = jnp.where(key_block == lax.broadcasted_iota(I32, (seq, LANES), 1),
                                         big, 0.0).astype(BF16)
        vb_ref[...] = v_ref[...].astype(BF16)

    qa = qa_ref[pl.ds(pl.multiple_of(qt * blk, blk), blk), :]

    def attend(nk, tail):
        s = _dot_nt(qa, ka_ref[0:nk, :])
        kpos = nk - tail + lax.broadcasted_iota(I32, (blk, tail), 1)
        tpos = qt * blk + lax.broadcasted_iota(I32, (blk, tail), 0)
        s_tail = jnp.where(kpos <= tpos, s[:, nk - tail:], NEG_INF)
        s = s_tail if tail == nk else jnp.concatenate([s[:, :nk - tail], s_tail], axis=1)
        p = jnp.exp(s - jnp.max(s, axis=-1, keepdims=True))
        l = jnp.sum(p, axis=-1, keepdims=True)
        o_ref[...] = _dot(p.astype(BF16), vb_ref[0:nk, :]) / l

    for v in range(n_ranges):
        lo, hi = v * nblk // n_ranges, (v + 1) * nblk // n_ranges

        @pl.when((qt >= lo) & (qt < hi))
        def _(lo=lo, hi=hi):
            attend(hi * blk, (hi - lo) * blk)


def _moba_prompt_attn(qkv, batch, seq, heads):
    blk = MOBA_BLOCK
    nblk = seq // blk
    assert nblk <= LANES
    topk = min(MOBA_TOPK, (seq - 1) // blk)
    n_ranges = 4 if nblk % 4 == 0 else 1
    body = functools.partial(_moba_prompt_body, seq=seq, nblk=nblk, blk=blk, topk=topk, n_ranges=n_ranges)
    return pl.pallas_call(
        body,
        grid=(batch, heads, nblk),
        in_specs=[
            pl.BlockSpec((seq, HEAD_DIM), lambda b, h, t: (b, h)),
            pl.BlockSpec((seq, HEAD_DIM), lambda b, h, t: (b, heads + h)),
            pl.BlockSpec((seq, HEAD_DIM), lambda b, h, t: (b, 2 * heads + h)),
        ],
        out_specs=pl.BlockSpec((blk, HEAD_DIM), lambda b, h, t: (b * nblk + t, h)),
        out_shape=jax.ShapeDtypeStruct((batch * seq, heads * HEAD_DIM), F32),
        scratch_shapes=[pltpu.VMEM((LANES, HEAD_DIM), F32), pltpu.VMEM((seq, 2 * HEAD_DIM), BF16),
                        pltpu.VMEM((seq, 2 * HEAD_DIM), BF16), pltpu.VMEM((seq, HEAD_DIM), BF16)],
        compiler_params=_cp("parallel", "parallel", "arbitrary"),
        name="moba_prompt",
    )(qkv, qkv, qkv)


def _moba_gate_body(pt_ref, k0_ref, k1_ref, q_ref, o_ref, gate_ref, *, nfull, topk):
    n = pl.program_id(1)
    lane = lax.broadcasted_iota(I32, gate_ref.shape, 1)

    @pl.when(n == 0)
    def _():
        gate_ref[...] = jnp.full(gate_ref.shape, -jnp.inf, F32)

    kmean = (jnp.sum(k0_ref[...], axis=0) + jnp.sum(k1_ref[...], axis=0)) / float(MOBA_BLOCK)
    g = jnp.sum(kmean * q_ref[...], axis=-1, keepdims=True)
    gate_ref[...] = jnp.where(lane == n, g, gate_ref[...])

    @pl.when(n == nfull - 1)
    def _():
        gate = gate_ref[...]
        lanef = lane.astype(F32)
        picks = jnp.zeros(gate.shape, F32)
        for r in range(topk):
            best = jnp.max(gate, axis=-1, keepdims=True)
            first = jnp.min(jnp.where(gate == best, lanef, float(LANES)), axis=-1, keepdims=True)
            picks = jnp.where(lane == r, first, picks)
            gate = jnp.where(lanef == first, -jnp.inf, gate)
        o_ref[...] = picks.astype(I32)


def _moba_sample_select(pool_k, q3, page_table_flat, nfull):
    bsz, heads, _ = q3.shape
    n_pages = page_table_flat.shape[0] // bsz
    topk = min(MOBA_TOPK, nfull)
    assert nfull <= LANES and MOBA_BLOCK == 2 * PAGE_SIZE
    body = functools.partial(_moba_gate_body, nfull=nfull, topk=topk)
    page = (None, PAGE_SIZE, heads, HEAD_DIM)
    grid_spec = pltpu.PrefetchScalarGridSpec(
        num_scalar_prefetch=1,
        grid=(bsz, nfull),
        in_specs=[
            pl.BlockSpec(page, lambda b, n, pt: (pt[b * n_pages + 2 * n], 0, 0, 0)),
            pl.BlockSpec(page, lambda b, n, pt: (pt[b * n_pages + 2 * n + 1], 0, 0, 0)),
            pl.BlockSpec((None, heads, HEAD_DIM), lambda b, n, pt: (b, 0, 0)),
        ],
        out_specs=pl.BlockSpec((None, heads, LANES), lambda b, n, pt: (b, 0, 0)),
        scratch_shapes=[pltpu.VMEM((heads, LANES), F32)],
    )
    return pl.pallas_call(
        body,
        grid_spec=grid_spec,
        out_shape=jax.ShapeDtypeStruct((bsz, heads, LANES), I32),
        compiler_params=_cp("parallel", "arbitrary"),
        name="moba_sample_select",
    )(page_table_flat, pool_k, pool_k, q3)


def _moba_sample_attn_body(pt_ref, sel_ref, q_ref, kn_ref, vn_ref, kp_ref, vp_ref, o_ref,
                           m_ref, l_ref, acc_ref, *, nsteps, heads):
    h = pl.program_id(1)
    r = pl.program_id(2)
    scale = HEAD_DIM ** -0.5
    q8 = jnp.broadcast_to(q_ref[...], (8, HEAD_DIM))

    @pl.when(r == 0)
    def _():
        s_own = jnp.sum(q8 * kn_ref[...], axis=-1, keepdims=True) * scale
        m_ref[...] = jnp.broadcast_to(s_own, m_ref.shape)
        l_ref[...] = jnp.ones(l_ref.shape, F32)
        acc_ref[...] = jnp.broadcast_to(vn_ref[...], acc_ref.shape)

    s = _dot_nt(q8.astype(BF16), kp_ref[...].astype(BF16)) * scale
    own = lax.broadcasted_iota(I32, s.shape, 1) % heads == h
    s = jnp.where(own, s, NEG_INF)
    m = m_ref[:, 0:1]
    m_new = jnp.maximum(m, jnp.max(s, axis=-1, keepdims=True))
    a = jnp.exp(m - m_new)
    p = jnp.exp(s - m_new)
    l_new = a * l_ref[:, 0:1] + jnp.sum(p, axis=-1, keepdims=True)
    acc_new = a * acc_ref[...] + _dot(p.astype(BF16), vp_ref[...].astype(BF16))
    m_ref[...] = jnp.broadcast_to(m_new, m_ref.shape)
    l_ref[...] = jnp.broadcast_to(l_new, l_ref.shape)
    acc_ref[...] = acc_new

    @pl.when(r == nsteps - 1)
    def _():
        o_ref[...] = (acc_new / l_new)[0:1, :]


def _moba_sample_attn(qkv3, pool_k3, pool_v3, page_table_flat, sel_flat, heads, topk):
    bsz = qkv3.shape[0]
    n_pages = page_table_flat.shape[0] // bsz
    nsteps = topk * 2

    def page(b, h, r, pt, sel):
        blk = sel[(b * heads + h) * topk + r // 2]
        return pt[b * n_pages + 2 * blk + r % 2], 0, 0

    grid_spec = pltpu.PrefetchScalarGridSpec(
        num_scalar_prefetch=2,
        grid=(bsz, heads, nsteps),
        in_specs=[
            pl.BlockSpec((None, 1, HEAD_DIM), lambda b, h, r, pt, sel: (b, 0, h)),
            pl.BlockSpec((None, 1, HEAD_DIM), lambda b, h, r, pt, sel: (b, 0, heads + h)),
            pl.BlockSpec((None, 1, HEAD_DIM), lambda b, h, r, pt, sel: (b, 0, 2 * heads + h)),
            pl.BlockSpec((None, PAGE_SIZE * heads, HEAD_DIM), page),
            pl.BlockSpec((None, PAGE_SIZE * heads, HEAD_DIM), page),
        ],
        out_specs=pl.BlockSpec((None, 1, HEAD_DIM), lambda b, h, r, pt, sel: (b, 0, h)),
        scratch_shapes=[pltpu.VMEM((8, LANES), F32), pltpu.VMEM((8, LANES), F32),
                        pltpu.VMEM((8, HEAD_DIM), F32)],
    )
    return pl.pallas_call(
        functools.partial(_moba_sample_attn_body, nsteps=nsteps, heads=heads),
        grid_spec=grid_spec,
        out_shape=jax.ShapeDtypeStruct((bsz, 1, heads * HEAD_DIM), F32),
        compiler_params=_cp("parallel", "parallel", "arbitrary"),
        name="moba_sample_attn",
    )(page_table_flat, sel_flat, qkv3, qkv3, qkv3, pool_k3, pool_v3)


def _pool_body(h_ref, prev_ref, w_ref, sc_ref, x_ref, o_ref, *, tm, tiles_per_seq, group):
    tile = pl.program_id(0) % tiles_per_seq
    h = h_ref[...]
    prev = jnp.where(tile == 0, 0.0, prev_ref[...])
    ext = jnp.concatenate([prev, h], axis=0)
    pos = tile * tm + lax.broadcasted_iota(I32, (tm, 1), 0)
    for g, w in enumerate(POOL_WINDOWS):
        sl = slice(g * group, (g + 1) * group)
        cur = ext[:, sl]
        shift = 1
        while shift < w:
            cur = cur + pltpu.roll(cur, shift, 0)
            shift *= 2
        cnt = jnp.minimum(pos + 1, w).astype(F32)
        pooled = cur[POOL_HALO:, :] / cnt - h[:, sl]
        y = _dot(pooled.astype(BF16), w_ref[g].astype(BF16))
        o_ref[:, sl] = x_ref[:, sl] + y * sc_ref[:, sl]


def _pool_mix(h, x, w_pool, scale, seq, *, tm):
    m, d = h.shape
    group = d // len(POOL_WINDOWS)
    tiles_per_seq = seq // tm
    halo_blocks = tm // POOL_HALO
    body = functools.partial(_pool_body, tm=tm, tiles_per_seq=tiles_per_seq, group=group)
    return pl.pallas_call(
        body,
        grid=(m // tm,),
        in_specs=[
            pl.BlockSpec((tm, d), lambda i: (i, 0)),
            pl.BlockSpec((POOL_HALO, d), lambda i: (jnp.maximum(i * halo_blocks - 1, 0), 0)),
            pl.BlockSpec((len(POOL_WINDOWS), group, group), lambda i: (0, 0, 0)),
            pl.BlockSpec((1, d), lambda i: (0, 0)),
            pl.BlockSpec((tm, d), lambda i: (i, 0)),
        ],
        out_specs=pl.BlockSpec((tm, d), lambda i: (i, 0)),
        out_shape=jax.ShapeDtypeStruct((m, d), F32),
        compiler_params=_cp("parallel"),
        name="pool_mix",
    )(h, h, w_pool, scale, x)


def _float_key(score):
    bits = lax.bitcast_convert_type(score, I32)
    return jnp.where(bits < 0, bits ^ 0x7FFFFFFF, bits)


def _count(mask, axes):
    ones = jnp.where(mask, 1.0, 0.0)
    for ax in (axes if isinstance(axes, tuple) else (axes,)):
        ones = jnp.sum(ones, axis=ax, keepdims=True)
    return ones


def _kth_largest_key(count_ge, k, shape):
    t = jnp.where(count_ge(jnp.zeros(shape, I32)) >= k, 0, INT_MIN).astype(I32)

    def body(i, t):
        cand = t | jnp.left_shift(jnp.int32(1), 30 - i)
        return jnp.where(count_ge(cand) >= k, cand, t)

    return lax.fori_loop(0, 31, body, t)


def _tie_limit(count_eq_below, need, shape, nbits):
    def body(i, j):
        cand = j | jnp.left_shift(jnp.int32(1), nbits - 1 - i)
        return jnp.where(count_eq_below(cand) <= need, cand, j)

    return lax.fori_loop(0, nbits, body, jnp.zeros(shape, I32))


def _dsa_prompt_body(q_ref, k_ref, v_ref, qi_ref, wq_ref, ki_ref, o_ref,
                     qs_ref, os_ref, bias_ref, lim_ref, *, seq, tq, ntop, heads, n_ranges):
    qt = pl.program_id(1)
    scale = HEAD_DIM ** -0.5
    group = heads // C_KV_HEADS
    nt = seq // tq

    qi = qi_ref[...]
    wi = wq_ref[:, IDX_DIM:IDX_DIM + IDX_HEADS] * (IDX_HEADS ** -0.5 * IDX_DIM ** -0.5)
    for h in range(heads):
        qs_ref[h] = (q_ref[:, h * HEAD_DIM:(h + 1) * HEAD_DIM] * scale).astype(BF16)

    def attend(nk):
        score = jnp.zeros((tq, nk), F32)
        for h in range(IDX_HEADS):
            d = _dot_nt(qi[:, h * IDX_DIM:(h + 1) * IDX_DIM].astype(BF16), ki_ref[0:nk, :])
            score = score + jnp.maximum(d, 0.0) * wi[:, h:h + 1]
        kpos = lax.broadcasted_iota(I32, (tq, nk), 1)
        tpos = qt * tq + lax.broadcasted_iota(I32, (tq, nk), 0)
        causal = kpos <= tpos
        key = _float_key(jnp.where(causal, score, NEG_INF))

        thr = _kth_largest_key(lambda c: _count(key >= c, -1), float(ntop), (tq, 1))
        above = _count(key > thr, -1)
        tied = key == thr
        need = float(ntop) - above
        lim_ref[...] = jnp.full((tq, 1), nk, I32)

        @pl.when(jnp.max(_count(tied, -1) - need) > 0.0)
        def _():
            lim_ref[...] = _tie_limit(lambda j: _count(tied & (kpos < j), -1), need, (tq, 1),
                                      nk.bit_length())

        chosen = (key > thr) | (tied & (kpos < lim_ref[...]))
        bias_ref[:, 0:nk] = jnp.where(chosen & causal, 0.0, NEG_INF)

        for g in range(C_KV_HEADS):
            sl = slice(g * HEAD_DIM, (g + 1) * HEAD_DIM)

            def head(j, carry, g=g, sl=sl):
                h = g * group + j
                s = _dot_nt(qs_ref[h], k_ref[0:nk, sl]) + bias_ref[:, 0:nk]
                p = jnp.exp(s - jnp.max(s, axis=-1, keepdims=True))
                l = jnp.sum(p, axis=-1, keepdims=True)
                os_ref[h] = _dot(p.astype(BF16), v_ref[0:nk, sl]) / l
                return carry

            lax.fori_loop(0, group, head, 0)

    for v in range(n_ranges):
        lo, hi = v * nt // n_ranges, (v + 1) * nt // n_ranges

        @pl.when((qt >= lo) & (qt < hi))
        def _(hi=hi):
            attend(hi * tq)

    for h in range(heads):
        o_ref[:, h * HEAD_DIM:(h + 1) * HEAD_DIM] = os_ref[h]


def _dsa_prompt_attn(proj, tail, k_bf, v_bf, ki_bf, batch, seq, heads):
    tq = DSA_Q_TILE
    nt = seq // tq
    ntop = min(DSA_TOPK, seq // 4)
    qw = heads * HEAD_DIM
    kvw = C_KV_HEADS * HEAD_DIM
    iw = IDX_HEADS * IDX_DIM
    assert (qw + 2 * kvw) % iw == 0
    n_ranges = 4 if nt % 4 == 0 else 1
    body = functools.partial(_dsa_prompt_body, seq=seq, tq=tq, ntop=ntop, heads=heads, n_ranges=n_ranges)
    return pl.pallas_call(
        body,
        grid=(batch, nt),
        in_specs=[
            pl.BlockSpec((tq, qw), lambda b, t: (b * nt + t, 0)),
            pl.BlockSpec((seq, kvw), lambda b, t: (b, 0)),
            pl.BlockSpec((seq, kvw), lambda b, t: (b, 0)),
            pl.BlockSpec((tq, iw), lambda b, t: (b * nt + t, (qw + 2 * kvw) // iw)),
            pl.BlockSpec((tq, tail.shape[1]), lambda b, t: (b * nt + t, 0)),
            pl.BlockSpec((seq, IDX_DIM), lambda b, t: (b, 0)),
        ],
        out_specs=pl.BlockSpec((tq, qw), lambda b, t: (b * nt + t, 0)),
        out_shape=jax.ShapeDtypeStruct((batch * seq, qw), F32),
        scratch_shapes=[pltpu.VMEM((heads, tq, HEAD_DIM), BF16), pltpu.VMEM((heads, tq, HEAD_DIM), F32),
                        pltpu.VMEM((tq, seq), F32), pltpu.VMEM((tq, 1), I32)],
        compiler_params=_cp("parallel", "arbitrary"),
        name="dsa_prompt",
    )(proj, k_bf, v_bf, proj, tail, ki_bf)


def _index_score_rows(qi, wi, ki):
    d = _dot_nt(qi.astype(BF16), ki.astype(BF16)) * (IDX_DIM ** -0.5)
    return jnp.sum(jnp.maximum(d, 0.0) * (wi * (IDX_HEADS ** -0.5)), axis=0, keepdims=True)


def _dsa_sample_scores_body(pt_ref, *refs, bsz):
    ki_refs = refs[:bsz]
    qi_ref, wi_ref, o_ref = refs[bsz:]
    p = pl.program_id(0)
    for b in range(bsz):
        o_ref[b, pl.ds(p, 1), :] = _index_score_rows(qi_ref[b], wi_ref[b], ki_refs[b][...])


def _dsa_sample_scores(pool_ki, qi3, wi3, page_table_flat):
    bsz = qi3.shape[0]
    n_pages = page_table_flat.shape[0] // bsz
    page_specs = [
        pl.BlockSpec((None, PAGE_SIZE, IDX_DIM), lambda p, pt, b=b: (pt[b * n_pages + p], 0, 0))
        for b in range(bsz)
    ]
    grid_spec = pltpu.PrefetchScalarGridSpec(
        num_scalar_prefetch=1,
        grid=(n_pages,),
        in_specs=page_specs + [
            pl.BlockSpec(qi3.shape, lambda p, pt: (0, 0, 0)),
            pl.BlockSpec(wi3.shape, lambda p, pt: (0, 0, 0)),
        ],
        out_specs=pl.BlockSpec((bsz, n_pages, PAGE_SIZE), lambda p, pt: (0, 0, 0)),
    )
    return pl.pallas_call(
        functools.partial(_dsa_sample_scores_body, bsz=bsz),
        grid_spec=grid_spec,
        out_shape=jax.ShapeDtypeStruct((bsz, n_pages, PAGE_SIZE), F32),
        compiler_params=_cp("arbitrary"),
        name="dsa_sample_scores",
    )(page_table_flat, *([pool_ki] * bsz), qi3, wi3)


def _dsa_sample_attn_body(pt_ref, sc_ref, qi_ref, wi_ref, kin_ref, q_ref, kn_ref, vn_ref, *refs,
                          n_pages, ntop, heads, pages_per_step):
    kp_refs = refs[:pages_per_step]
    vp_refs = refs[pages_per_step:2 * pages_per_step]
    o_ref, mask_ref, new_ref, m_ref, l_ref, acc_ref = refs[2 * pages_per_step:]
    p = pl.program_id(1)
    scale = HEAD_DIM ** -0.5
    group = heads // C_KV_HEADS
    past = n_pages * PAGE_SIZE

    @pl.when(p == 0)
    def _():
        key = _float_key(sc_ref[...])
        d_new = jnp.sum(qi_ref[...] * kin_ref[...], axis=-1, keepdims=True) * (IDX_DIM ** -0.5)
        key_new = _float_key(jnp.sum(jnp.maximum(d_new, 0.0) * (wi_ref[...] * (IDX_HEADS ** -0.5)),
                                     axis=0, keepdims=True))
        pos = (lax.broadcasted_iota(I32, key.shape, 0) * PAGE_SIZE
               + lax.broadcasted_iota(I32, key.shape, 1))

        def count_ge(c):
            return _count(key >= c, (0, 1)) + jnp.where(key_new >= c, 1.0, 0.0)

        thr = _kth_largest_key(count_ge, float(ntop), (1, 1))
        above = _count(key > thr, (0, 1)) + jnp.where(key_new > thr, 1.0, 0.0)
        tied = key == thr
        tied_new = key_new == thr
        need = float(ntop) - above

        def count_eq_below(j):
            return _count(tied & (pos < j), (0, 1)) + jnp.where(tied_new & (past < j), 1.0, 0.0)

        lim = _tie_limit(count_eq_below, need, (1, 1), (past + 1).bit_length())
        mask_ref[...] = jnp.where((key > thr) | (tied & (pos < lim)), 1.0, 0.0)
        new_ref[...] = jnp.broadcast_to(
            jnp.where((key_new > thr) | (tied_new & (past < lim)), 1.0, 0.0), new_ref.shape)
        m_ref[...] = jnp.full(m_ref.shape, NEG_INF, F32)
        l_ref[...] = jnp.zeros(l_ref.shape, F32)
        acc_ref[...] = jnp.zeros(acc_ref.shape, F32)

    q = q_ref[...]
    qb = q.astype(BF16)
    kv_head = lax.broadcasted_iota(I32, (heads, 1), 0) // group

    def per_kv_head(f):
        out = f(0)
        for g in range(1, C_KV_HEADS):
            out = jnp.where(kv_head == g, f(g), out)
        return out

    def page_scores(j):
        s_j = per_kv_head(lambda g: _dot_nt(qb, kp_refs[j][:, g, :].astype(BF16))) * scale
        return jnp.where(mask_ref[pl.ds(p * pages_per_step + j, 1), :] > 0.0, s_j, NEG_INF)

    s = jnp.concatenate([page_scores(j) for j in range(pages_per_step)], axis=1)
    m = m_ref[...]
    m_new = jnp.maximum(m, jnp.max(s, axis=-1, keepdims=True))
    a = jnp.exp(m - m_new)
    pr = jnp.where(s > NEG_INF, jnp.exp(s - m_new), 0.0)
    prb = pr.astype(BF16)
    pv = jnp.zeros(acc_ref.shape, F32)
    for j in range(pages_per_step):
        pr_j = prb[:, j * PAGE_SIZE:(j + 1) * PAGE_SIZE]
        pv = pv + per_kv_head(lambda g: _dot(pr_j, vp_refs[j][:, g, :].astype(BF16)))
    l_new = a * l_ref[...] + jnp.sum(pr, axis=-1, keepdims=True)
    acc_new = a * acc_ref[...] + pv
    m_ref[...] = m_new
    l_ref[...] = l_new
    acc_ref[...] = acc_new

    @pl.when(p == n_pages // pages_per_step - 1)
    def _():
        s_new = jnp.sum(q * kn_ref[...], axis=-1, keepdims=True) * scale
        take = new_ref[:, 0:1] > 0.0
        s_new = jnp.where(take, s_new, NEG_INF)
        m_fin = jnp.maximum(m_new, s_new)
        a2 = jnp.exp(m_new - m_fin)
        p_new = jnp.where(take, jnp.exp(s_new - m_fin), 0.0)
        l_fin = a2 * l_new + p_new
        o_ref[...] = (a2 * acc_new + p_new * vn_ref[...]) / l_fin


def _dsa_sample_attn(scores, qi3, wi3, kin3, q3, kn3, vn3, pool_k, pool_v, page_table_flat, heads):
    bsz, n_pages, _ = scores.shape
    ntop = min(DSA_TOPK, (n_pages * PAGE_SIZE + 1) // 4)
    pps = 8 if n_pages % 8 == 0 else 1
    per_b = lambda shape: pl.BlockSpec((None,) + shape, lambda b, p, pt: (b, 0, 0))
    pages = [pl.BlockSpec((None, PAGE_SIZE, C_KV_HEADS, HEAD_DIM),
                          lambda b, p, pt, j=j: (pt[b * n_pages + p * pps + j], 0, 0, 0))
             for j in range(pps)]
    grid_spec = pltpu.PrefetchScalarGridSpec(
        num_scalar_prefetch=1,
        grid=(bsz, n_pages // pps),
        in_specs=[per_b((n_pages, PAGE_SIZE)), per_b((IDX_HEADS, IDX_DIM)), per_b((IDX_HEADS, 1)),
                  per_b((1, IDX_DIM)), per_b((heads, HEAD_DIM)), per_b((heads, HEAD_DIM)),
                  per_b((heads, HEAD_DIM))] + pages + pages,
        out_specs=per_b((heads, HEAD_DIM)),
        scratch_shapes=[pltpu.VMEM((n_pages, PAGE_SIZE), F32), pltpu.VMEM((heads, LANES), F32),
                        pltpu.VMEM((heads, 1), F32), pltpu.VMEM((heads, 1), F32),
                        pltpu.VMEM((heads, HEAD_DIM), F32)],
    )
    body = functools.partial(_dsa_sample_attn_body, n_pages=n_pages, ntop=ntop, heads=heads,
                             pages_per_step=pps)
    return pl.pallas_call(
        body,
        grid_spec=grid_spec,
        out_shape=jax.ShapeDtypeStruct((bsz, heads, HEAD_DIM), F32),
        compiler_params=_cp("parallel", "arbitrary"),
        name="dsa_sample_attn",
    )(page_table_flat, scores, qi3, wi3, kin3, q3, kn3, vn3, *([pool_k] * pps), *([pool_v] * pps))


def _gla_body(q_ref, k_ref, v_ref, r_ref, low_ref, wg2_ref, bg_ref, gn_ref, s0_ref, o_ref, sf_ref,
              state_ref, *, chunk, n_valid, dk, dv):
    c = pl.program_id(2)

    @pl.when(c == 0)
    def _():
        state_ref[...] = s0_ref[...]

    x = _dot_hi(low_ref[...], wg2_ref[...]) + bg_ref[...]
    g = (jnp.minimum(x, 0.0) - jnp.log(1.0 + jnp.exp(-jnp.abs(x)))) / GLA_TAU
    row = lax.broadcasted_iota(I32, (chunk, 1), 0)
    if n_valid < chunk:
        g = jnp.where(row < n_valid, g, 0.0)
    tri = (lax.broadcasted_iota(I32, (chunk, chunk), 1)
           <= lax.broadcasted_iota(I32, (chunk, chunk), 0)).astype(BF16)
    bcum = _dot_exact_lhs(tri, g)
    q = q_ref[...] * (dk ** -0.5)
    k = k_ref[...]
    vb = v_ref[...].astype(BF16)
    sub = 8
    lane = lax.broadcasted_iota(I32, (sub, chunk), 1)
    row_groups = [jnp.zeros((sub, chunk), F32) for _ in range(chunk // sub)]
    for s in range(chunk):
        r0 = (s // sub) * sub
        rows = r0 + lax.broadcasted_iota(I32, (chunk - r0, 1), 0)
        decay = jnp.exp(jnp.where(rows >= s, bcum[r0:, :] - bcum[s:s + 1, :], -jnp.inf))
        col = jnp.sum(q[r0:, :] * k[s:s + 1, :] * decay, axis=-1, keepdims=True)
        for rg in range(s // sub, chunk // sub):
            part = col[rg * sub - r0:(rg + 1) * sub - r0, :]
            row_groups[rg] = jnp.where(lane == s, part, row_groups[rg])
    att = jnp.concatenate(row_groups, axis=0)
    state = state_ref[...]
    o = _dot(att.astype(BF16), vb) + _dot((q * jnp.exp(bcum)).astype(BF16), state.astype(BF16))

    b_last = bcum[chunk - 1:chunk, :]
    last_row = (lax.broadcasted_iota(I32, (chunk, dv), 0) == chunk - 1).astype(BF16)
    b_last_cols = _dot_exact_rhs(bcum, last_row, _dot_tn)
    kd = (k * jnp.exp(b_last - bcum)).astype(BF16)
    new_state = jnp.exp(b_last_cols) * state + _dot_tn(kd, vb)
    state_ref[...] = new_state

    @pl.when(c == pl.num_programs(2) - 1)
    def _():
        sf_ref[...] = new_state

    o_ref[...] = _rms(o) * gn_ref[...] * _silu(r_ref[...])


def _gla(proj3, low3, w_g2, b_g, g_n, s0, *, chunk, n_valid):
    bsz, length, _ = proj3.shape
    heads = GLA_HEADS
    dk, dv = s0.shape[2], s0.shape[3]
    assert dv == 2 * dk
    body = functools.partial(_gla_body, chunk=chunk, n_valid=n_valid, dk=dk, dv=dv)
    return pl.pallas_call(
        body,
        grid=(bsz, heads, length // chunk),
        in_specs=[
            pl.BlockSpec((None, chunk, dk), lambda b, h, c: (b, c, h)),
            pl.BlockSpec((None, chunk, dk), lambda b, h, c: (b, c, heads + h)),
            pl.BlockSpec((None, chunk, dv), lambda b, h, c: (b, c, heads + h)),
            pl.BlockSpec((None, chunk, dv), lambda b, h, c: (b, c, 2 * heads + h)),
            pl.BlockSpec((None, chunk, low3.shape[2]), lambda b, h, c: (b, c, 0)),
            pl.BlockSpec((low3.shape[2], dk), lambda b, h, c: (0, h)),
            pl.BlockSpec((1, dk), lambda b, h, c: (0, h)),
            pl.BlockSpec((1, dv), lambda b, h, c: (0, 0)),
            pl.BlockSpec((None, None, dk, dv), lambda b, h, c: (b, h, 0, 0)),
        ],
        out_specs=[
            pl.BlockSpec((None, chunk, dv), lambda b, h, c: (b, c, h)),
            pl.BlockSpec((None, None, dk, dv), lambda b, h, c: (b, h, 0, 0)),
        ],
        out_shape=[jax.ShapeDtypeStruct((bsz, length, heads * dv), F32),
                   jax.ShapeDtypeStruct(s0.shape, F32)],
        scratch_shapes=[pltpu.VMEM((dk, dv), F32)],
        compiler_params=_cp("parallel", "parallel", "arbitrary"),
        name="gla",
    )(proj3, proj3, proj3, proj3, low3, w_g2, b_g, g_n, s0)


def _mixer_a(xp, xs, batch, seq, norm_mix, layer, cache_k, cache_v, page_table, w_qkv, w_o, g_q, g_k):
    d = xp.shape[1]
    heads = d // HEAD_DIM
    gain = jnp.concatenate([jnp.tile(g_q, heads), jnp.tile(g_k, heads), jnp.ones((d,), F32)])[None, :]

    def qkv(x):
        return _proj(x, w_qkv, norm_gain=norm_mix, norm_layer=layer, head_gain=gain, n_head_norm_cols=2 * d)

    qkv_p = qkv(xp)
    o_p = _moba_prompt_attn(qkv_p, batch, seq, heads)
    xp = _proj(o_p, w_o, residual=xp)

    bsz = xs.shape[0]
    n_pages = page_table.shape[1]
    past = n_pages * PAGE_SIZE
    assert past % MOBA_BLOCK == 0
    nfull = past // MOBA_BLOCK
    topk = min(MOBA_TOPK, nfull)
    assert topk > 0
    qkv_s = qkv(xs)
    qkv_s3 = qkv_s[:, None, :]
    pool_k3 = cache_k.reshape(cache_k.shape[0], PAGE_SIZE * heads, HEAD_DIM)
    pool_v3 = cache_v.reshape(cache_v.shape[0], PAGE_SIZE * heads, HEAD_DIM)
    pt_flat = page_table.reshape(-1)
    sel = _moba_sample_select(cache_k, qkv_s[:, :d].reshape(bsz, heads, HEAD_DIM), pt_flat, nfull)
    sel_flat = sel[:, :, :topk].reshape(-1)
    o_s = _moba_sample_attn(qkv_s3, pool_k3, pool_v3, pt_flat, sel_flat, heads, topk)
    xs = _proj(o_s[:, 0, :], w_o, residual=xs)

    kv_shape = lambda t, n: t.reshape(n, -1, heads, HEAD_DIM)
    outs = (kv_shape(qkv_p[:, d:2 * d], batch), kv_shape(qkv_p[:, 2 * d:], batch),
            kv_shape(qkv_s[:, d:2 * d], bsz), kv_shape(qkv_s[:, 2 * d:], bsz))
    return xp, xs, outs


def _mixer_b(xp, xs, batch, seq, norm_mix, layer, state_pool, w_pool, scale):
    d = xp.shape[1]
    keep = state_pool.shape[1]
    scale = scale[None, :]
    hp = _rmsnorm(xp, norm_mix, layer)
    xp = _pool_mix(hp, xp, w_pool, scale, seq, tm=512)
    pool_p = hp.reshape(batch, seq, d)[:, seq - keep:]

    bsz = xs.shape[0]
    hs = _rmsnorm(xs, norm_mix, layer)
    ext = jnp.concatenate([state_pool, hs[:, None, :]], axis=1)
    assert keep + 1 == POOL_HALO
    x_ext = jnp.concatenate([jnp.zeros_like(state_pool), xs[:, None, :]], axis=1)
    y = _pool_mix(ext.reshape(bsz * POOL_HALO, d), x_ext.reshape(bsz * POOL_HALO, d), w_pool, scale,
                  POOL_HALO, tm=POOL_HALO)
    xs = y.reshape(bsz, POOL_HALO, d)[:, -1]
    return xp, xs, (pool_p, ext[:, 1:])


def _mixer_c(xp, xs, batch, seq, norm_mix, layer, cache_k, cache_v, cache_ki, page_table,
             w_in, w_o, g_q, g_k):
    d = xp.shape[1]
    heads = d // HEAD_DIM
    kvw = C_KV_HEADS * HEAD_DIM
    iw = IDX_HEADS * IDX_DIM
    main = d + 2 * kvw + iw
    w_main, w_tail = w_in[:, :main], w_in[:, main:]
    gain = jnp.concatenate([jnp.tile(g_q, heads), jnp.tile(g_k, C_KV_HEADS),
                            jnp.ones((main - d - kvw,), F32)])[None, :]

    def project(x):
        pm = _proj(x, w_main, norm_gain=norm_mix, norm_layer=layer, head_gain=gain, n_head_norm_cols=d + kvw)
        pt = _proj(x, w_tail, norm_gain=norm_mix, norm_layer=layer)
        return pm, pt

    pm, pt = project(xp)
    k_p, v_p, ki_p = pm[:, d:d + kvw], pm[:, d + kvw:d + 2 * kvw], pt[:, :IDX_DIM]
    o_p = _dsa_prompt_attn(pm, pt, k_p.astype(BF16), v_p.astype(BF16), ki_p.astype(BF16), batch, seq, heads)
    xp = _proj(o_p, w_o, residual=xp)

    bsz = xs.shape[0]
    group = heads // C_KV_HEADS
    sm, st = project(xs)
    k_s, v_s, ki_s = sm[:, d:d + kvw], sm[:, d + kvw:d + 2 * kvw], st[:, :IDX_DIM]
    qi3 = sm[:, d + 2 * kvw:].reshape(bsz, IDX_HEADS, IDX_DIM)
    wi3 = st[:, IDX_DIM:IDX_DIM + IDX_HEADS].reshape(bsz, IDX_HEADS, 1)
    pt_flat = page_table.reshape(-1)
    scores = _dsa_sample_scores(cache_ki, qi3, wi3, pt_flat)
    per_q_head = lambda t: jnp.repeat(t.reshape(bsz, C_KV_HEADS, HEAD_DIM), group, axis=1)
    o_s = _dsa_sample_attn(scores, qi3, wi3, ki_s[:, None, :], sm[:, :d].reshape(bsz, heads, HEAD_DIM),
                           per_q_head(k_s), per_q_head(v_s), cache_k, cache_v, pt_flat, heads)
    xs = _proj(o_s.reshape(bsz, d), w_o, residual=xs)

    kv4 = lambda t, n: t.reshape(n, -1, C_KV_HEADS, HEAD_DIM)
    outs = (kv4(k_p, batch), kv4(v_p, batch), ki_p.reshape(batch, seq, IDX_DIM),
            kv4(k_s, bsz), kv4(v_s, bsz), ki_s.reshape(bsz, 1, IDX_DIM))
    return xp, xs, outs


def _mixer_d(xp, xs, batch, seq, norm_mix, layer, state, w_in, w_g2, b_g, g_n, w_o):
    d = xp.shape[1]
    dk, dv = state.shape[2], state.shape[3]
    main = 2 * GLA_HEADS * dk + 2 * GLA_HEADS * dv
    w_main, w_tail = w_in[:, :main], w_in[:, main:]

    def project(x):
        pm = _proj(x, w_main, norm_gain=norm_mix, norm_layer=layer)
        pt = _proj(x, w_tail, norm_gain=norm_mix, norm_layer=layer)
        return pm, pt

    pm, pt = project(xp)
    zero_state = jnp.zeros((batch,) + state.shape[1:], F32)
    o_p, s_p = _gla(pm.reshape(batch, seq, main), pt.reshape(batch, seq, -1), w_g2, b_g[None, :],
                    g_n[None, :], zero_state, chunk=min(GLA_CHUNK, seq), n_valid=min(GLA_CHUNK, seq))
    xp = _proj(o_p.reshape(batch * seq, GLA_HEADS * dv), w_o, residual=xp)

    bsz = xs.shape[0]
    sm, st = project(xs)
    pad = lambda t: jnp.pad(t[:, None, :], ((0, 0), (0, GLA_SAMPLE_ROWS - 1), (0, 0)))
    o_s, s_s = _gla(pad(sm), pad(st), w_g2, b_g[None, :], g_n[None, :], state,
                    chunk=GLA_SAMPLE_ROWS, n_valid=1)
    xs = _proj(o_s[:, 0, :], w_o, residual=xs)
    return xp, xs, (s_p, s_s)


def kernel(x_prompt, x_sample, cache_a_k, cache_a_v, state_b_pool, cache_c_k, cache_c_v, cache_c_idx_k,
           state_d_gla, page_table, norm_ffn1, ffn1_w_gate, ffn1_w_up, ffn1_w_down, norm_mix, norm_ffn2,
           ffn2_w_gate, ffn2_w_up, ffn2_w_down, a_w_qkv, a_w_o, a_g_q, a_g_k, b_w_pool, b_scale, c_w_in,
           c_w_o, c_g_q, c_g_k, d_w_in, d_w_g2, d_b_g, d_g_n, d_w_o):
    batch, seq, d = x_prompt.shape
    bsz = x_sample.shape[0]
    assert x_sample.shape[1] == 1
    xp = x_prompt.reshape(batch * seq, d)
    xs = x_sample.reshape(bsz, d)
    depth = norm_ffn1.shape[0]
    ffn1_w_gate, ffn1_w_up, ffn1_w_down, ffn2_w_gate, ffn2_w_up, ffn2_w_down = (
        w.astype(BF16) for w in (ffn1_w_gate, ffn1_w_up, ffn1_w_down, ffn2_w_gate, ffn2_w_up, ffn2_w_down))
    outs = {}
    for i in range(depth):
        xp = _ffn(xp, norm_ffn1, ffn1_w_gate, ffn1_w_up, ffn1_w_down, i)
        xs = _ffn(xs, norm_ffn1, ffn1_w_gate, ffn1_w_up, ffn1_w_down, i)
        m = i % 4
        if m == 0:
            xp, xs, outs["a"] = _mixer_a(xp, xs, batch, seq, norm_mix, i, cache_a_k, cache_a_v, page_table,
                                         a_w_qkv, a_w_o, a_g_q, a_g_k)
        elif m == 1:
            xp, xs, outs["b"] = _mixer_b(xp, xs, batch, seq, norm_mix, i, state_b_pool, b_w_pool, b_scale)
        elif m == 2:
            xp, xs, outs["c"] = _mixer_c(xp, xs, batch, seq, norm_mix, i, cache_c_k, cache_c_v, cache_c_idx_k,
                                         page_table, c_w_in, c_w_o, c_g_q, c_g_k)
        else:
            xp, xs, outs["d"] = _mixer_d(xp, xs, batch, seq, norm_mix, i, state_d_gla, d_w_in, d_w_g2,
                                         d_b_g, d_g_n, d_w_o)
        xp = _ffn(xp, norm_ffn2, ffn2_w_gate, ffn2_w_up, ffn2_w_down, i)
        xs = _ffn(xs, norm_ffn2, ffn2_w_gate, ffn2_w_up, ffn2_w_down, i)
    return (xp.reshape(batch, seq, d), xs.reshape(bsz, 1, d)) + outs["a"] + outs["b"] + outs["c"] + outs["d"]
```

```python
import functools

import jax
import jax.numpy as jnp
from jax import lax
from jax.experimental import pallas as pl
from jax.experimental.pallas import tpu as pltpu

F32 = jnp.float32
BF16 = jnp.bfloat16
I32 = jnp.int32

NORM_EPS = 1e-6
NEG_INF = -1e30
HEAD_DIM = 128
LANES = 128
SUBLANES = 8
PAGE_SIZE = 128
MOBA_BLOCK = 256
MOBA_TOPK = 3
POOL_WINDOWS = (2, 4, 8, 16)
POOL_HALO = 16
C_KV_HEADS = 4
IDX_HEADS = 16
IDX_DIM = 64
DSA_TOPK = 256
DSA_Q_TILE = 128
GLA_HEADS = 4
GLA_GATE_RANK = 16
GLA_TAU = 16.0
GLA_CHUNK = 64
GLA_SAMPLE_ROWS = 16
VMEM_LIMIT = 56 * 1024 * 1024

INT_MIN = -2147483648


def _cp(*sem):
    return pltpu.CompilerParams(dimension_semantics=sem, vmem_limit_bytes=VMEM_LIMIT)


ROW_TILE = 1024
COL_TILE = 512
FFN_OUT_TILE = 256


def _row_tile(m):
    return ROW_TILE if m % ROW_TILE == 0 else m


FFN_HIDDEN_TILE = 512


def _dot(a, b):
    return jnp.dot(a, b, preferred_element_type=F32)


def _dot_nt(a, b):
    return lax.dot_general(a, b, (((1,), (1,)), ((), ())), preferred_element_type=F32)


def _dot_tn(a, b):
    return lax.dot_general(a, b, (((0,), (0,)), ((), ())), preferred_element_type=F32)


def _split3(a):
    a1 = a.astype(BF16)
    r1 = a - a1.astype(F32)
    a2 = r1.astype(BF16)
    a3 = (r1 - a2.astype(F32)).astype(BF16)
    return a1, a2, a3


def _dot_hi(a, b, dot=_dot):
    a1, a2, _ = _split3(a)
    b1, b2, _ = _split3(b)
    return dot(a1, b1) + (dot(a1, b2) + dot(a2, b1))


def _dot_exact_lhs(a01, b, dot=_dot):
    b1, b2, b3 = _split3(b)
    return dot(a01, b1) + (dot(a01, b2) + dot(a01, b3))


def _dot_exact_rhs(a, b01, dot=_dot):
    a1, a2, a3 = _split3(a)
    return dot(a1, b01) + (dot(a2, b01) + dot(a3, b01))


def _rms(x):
    return x * lax.rsqrt(jnp.mean(x * x, axis=-1, keepdims=True) + NORM_EPS)


def _silu(x):
    return x * jax.nn.sigmoid(x)


def _ffn_body(x_ref, g_ref, wg_ref, wu_ref, wd_ref, res_ref, o_ref, xn_ref, h_ref, *, n_up, tf):
    j = pl.program_id(1)

    @pl.when(j == 0)
    def _():
        xn_ref[...] = (_rms(x_ref[...]) * g_ref[...]).astype(BF16)

    @pl.when(j < n_up)
    def _():
        xn = xn_ref[...]
        a = _dot(xn, wg_ref[...])
        u = _dot(xn, wu_ref[...])
        hidden = (_silu(a) * u).astype(BF16)
        for c in range(n_up):
            @pl.when(j == c)
            def _(c=c):
                h_ref[:, c * tf:(c + 1) * tf] = hidden

    @pl.when(j >= n_up)
    def _():
        o_ref[...] = res_ref[...] + 0.5 * _dot(h_ref[...], wd_ref[...])


def _ffn(x, g, wg, wu, wd, layer):
    m, d = x.shape
    f = wg.shape[-1]
    tm = _row_tile(m)
    tf = FFN_HIDDEN_TILE
    tn = FFN_OUT_TILE
    n_up = f // tf
    up = lambda j: jnp.minimum(j, n_up - 1)
    down = lambda j: jnp.maximum(j - n_up, 0)
    return pl.pallas_call(
        functools.partial(_ffn_body, n_up=n_up, tf=tf),
        grid=(m // tm, n_up + d // tn),
        in_specs=[
            pl.BlockSpec((tm, d), lambda i, j: (i, 0), pipeline_mode=pl.Buffered(1)),
            pl.BlockSpec((None, 1, d), lambda i, j: (layer, 0, 0)),
            pl.BlockSpec((None, d, tf), lambda i, j: (layer, 0, up(j))),
            pl.BlockSpec((None, d, tf), lambda i, j: (layer, 0, up(j))),
            pl.BlockSpec((None, f, tn), lambda i, j: (layer, 0, down(j))),
            pl.BlockSpec((tm, tn), lambda i, j: (i, down(j))),
        ],
        out_specs=pl.BlockSpec((tm, tn), lambda i, j: (i, down(j))),
        out_shape=jax.ShapeDtypeStruct((m, d), F32),
        scratch_shapes=[pltpu.VMEM((tm, d), BF16), pltpu.VMEM((tm, f), BF16)],
        compiler_params=_cp("parallel", "arbitrary"),
        name="ffn",
    )(x, g[:, None, :], wg, wu, wd, x)


def _proj_body(*refs, pre_norm, n_head_norm_tiles, residual, tn):
    it = iter(refs)
    x_ref = next(it)
    g_ref = next(it) if pre_norm else None
    w_ref = next(it)
    cg_ref = next(it) if n_head_norm_tiles else None
    r_ref = next(it) if residual else None
    o_ref = next(it)
    xn_ref = next(it)
    j = pl.program_id(1)

    @pl.when(j == 0)
    def _():
        x = x_ref[...]
        if pre_norm:
            x = _rms(x) * g_ref[...]
        xn_ref[...] = x.astype(BF16)

    y = _dot(xn_ref[...], w_ref[...].astype(BF16))
    if residual:
        y = r_ref[...] + y

    if n_head_norm_tiles:
        @pl.when(j < n_head_norm_tiles)
        def _():
            for c in range(tn // HEAD_DIM):
                sl = slice(c * HEAD_DIM, (c + 1) * HEAD_DIM)
                o_ref[:, sl] = _rms(y[:, sl]) * cg_ref[:, sl]

        @pl.when(j >= n_head_norm_tiles)
        def _():
            o_ref[...] = y
    else:
        o_ref[...] = y


def _proj(x, w, *, norm_gain=None, norm_layer=0, head_gain=None, n_head_norm_cols=0, residual=None):
    m, k = x.shape
    n = w.shape[1]
    tm = _row_tile(m)
    tn = COL_TILE if n % COL_TILE == 0 else n
    assert n_head_norm_cols % tn == 0
    n_head_norm_tiles = n_head_norm_cols // tn
    pre_norm = norm_gain is not None
    args = [x]
    specs = [pl.BlockSpec((tm, k), lambda i, j: (i, 0))]
    if pre_norm:
        args.append(norm_gain[:, None, :])
        specs.append(pl.BlockSpec((None, 1, k), lambda i, j: (norm_layer, 0, 0)))
    args.append(w)
    specs.append(pl.BlockSpec((k, tn), lambda i, j: (0, j)))
    if n_head_norm_tiles:
        args.append(head_gain)
        specs.append(pl.BlockSpec((1, tn), lambda i, j: (0, j)))
    if residual is not None:
        args.append(residual)
        specs.append(pl.BlockSpec((tm, tn), lambda i, j: (i, j)))
    body = functools.partial(_proj_body, pre_norm=pre_norm, n_head_norm_tiles=n_head_norm_tiles,
                             residual=residual is not None, tn=tn)
    return pl.pallas_call(
        body,
        grid=(m // tm, n // tn),
        in_specs=specs,
        out_specs=pl.BlockSpec((tm, tn), lambda i, j: (i, j)),
        out_shape=jax.ShapeDtypeStruct((m, n), F32),
        scratch_shapes=[pltpu.VMEM((tm, k), BF16)],
        compiler_params=_cp("parallel", "arbitrary"),
        name="proj",
    )(*args)


def _rmsnorm_body(x_ref, g_ref, o_ref):
    o_ref[...] = _rms(x_ref[...]) * g_ref[...]


def _rmsnorm(x, gains, layer):
    m, d = x.shape
    tm = _row_tile(m)
    return pl.pallas_call(
        _rmsnorm_body,
        grid=(m // tm,),
        in_specs=[pl.BlockSpec((tm, d), lambda i: (i, 0)),
                  pl.BlockSpec((None, 1, d), lambda i: (layer, 0, 0))],
        out_specs=pl.BlockSpec((tm, d), lambda i: (i, 0)),
        out_shape=jax.ShapeDtypeStruct((m, d), F32),
        compiler_params=_cp("parallel"),
        name="rmsnorm",
    )(x, gains[:, None, :])


def _moba_prompt_body(q_ref, k_ref, v_ref, o_ref, kmean_ref, qa_ref, ka_ref, vb_ref, *,
                      seq, nblk, blk, topk, n_ranges):
    qt = pl.program_id(2)
    scale = HEAD_DIM ** -0.5
    big = -NEG_INF

    @pl.when(qt == 0)
    def _():
        kmean_ref[...] = jnp.zeros(kmean_ref.shape, F32)
        for n in range(nblk):
            kmean_ref[n:n + 1, :] = jnp.mean(k_ref[n * blk:(n + 1) * blk, :], axis=0, keepdims=True)
        q = q_ref[...]
        gate = _dot_hi(q, kmean_ref[...], _dot_nt)
        col = lax.broadcasted_iota(I32, gate.shape, 1)
        colf = col.astype(F32)
        own = lax.broadcasted_iota(I32, gate.shape, 0) // blk
        gate = jnp.where(col < own, gate, NEG_INF)
        picked = jnp.zeros(gate.shape, F32)
        for _ in range(topk):
            best = jnp.max(gate, axis=-1, keepdims=True)
            first = jnp.min(jnp.where(gate == best, colf, float(LANES)), axis=-1, keepdims=True)
            hit = colf == first
            picked = jnp.where(hit, 1.0, picked)
            gate = jnp.where(hit, -jnp.inf, gate)
        picked = jnp.where(col < own, picked, jnp.where(col == own, 1.0, 0.0))
        qa_ref[:, 0:HEAD_DIM] = (q * scale).astype(BF16)
        qa_ref[:, HEAD_DIM:] = (picked - 1.0).astype(BF16)
        key_block = lax.broadcasted_iota(I32, (seq, LANES), 0) // blk
        ka_ref[:, 0:HEAD_DIM] = k_ref[...].astype(BF16)
        ka_ref[:, HEAD_DIM:] = jnp.where(key_block == lax.broadcasted_iota(I32, (seq, LANES), 1),
                                         big, 0.0).astype(BF16)
        vb_ref[...] = v_ref[...].astype(BF16)

    qa = qa_ref[pl.ds(pl.multiple_of(qt * blk, blk), blk), :]

    def attend(nk, tail):
        s = _dot_nt(qa, ka_ref[0:nk, :])
        kpos = nk - tail + lax.broadcasted_iota(I32, (blk, tail), 1)
        tpos = qt * blk + lax.broadcasted_iota(I32, (blk, tail), 0)
        s_tail = jnp.where(kpos <= tpos, s[:, nk - tail:], NEG_INF)
        s = s_tail if tail == nk else jnp.concatenate([s[:, :nk - tail], s_tail], axis=1)
        p = jnp.exp(s - jnp.max(s, axis=-1, keepdims=True))
        l = jnp.sum(p, axis=-1, keepdims=True)
        o_ref[...] = _dot(p.astype(BF16), vb_ref[0:nk, :]) / l

    for v in range(n_ranges):
        lo, hi = v * nblk // n_ranges, (v + 1) * nblk // n_ranges

        @pl.when((qt >= lo) & (qt < hi))
        def _(lo=lo, hi=hi):
            attend(hi * blk, (hi - lo) * blk)


def _moba_prompt_attn(qkv, batch, seq, heads):
    blk = MOBA_BLOCK
    nblk = seq // blk
    assert nblk <= LANES
    topk = min(MOBA_TOPK, (seq - 1) // blk)
    n_ranges = 4 if nblk % 4 == 0 else 1
    body = functools.partial(_moba_prompt_body, seq=seq, nblk=nblk, blk=blk, topk=topk, n_ranges=n_ranges)
    return pl.pallas_call(
        body,
        grid=(batch, heads, nblk),
        in_specs=[
            pl.BlockSpec((seq, HEAD_DIM), lambda b, h, t: (b, h)),
            pl.BlockSpec((seq, HEAD_DIM), lambda b, h, t: (b, heads + h)),
            pl.BlockSpec((seq, HEAD_DIM), lambda b, h, t: (b, 2 * heads + h)),
        ],
        out_specs=pl.BlockSpec((blk, HEAD_DIM), lambda b, h, t: (b * nblk + t, h)),
        out_shape=jax.ShapeDtypeStruct((batch * seq, heads * HEAD_DIM), F32),
        scratch_shapes=[pltpu.VMEM((LANES, HEAD_DIM), F32), pltpu.VMEM((seq, 2 * HEAD_DIM), BF16),
                        pltpu.VMEM((seq, 2 * HEAD_DIM), BF16), pltpu.VMEM((seq, HEAD_DIM), BF16)],
        compiler_params=_cp("parallel", "parallel", "arbitrary"),
        name="moba_prompt",
    )(qkv, qkv, qkv)


def _moba_gate_body(pt_ref, *refs, nfull, topk, blocks_per_step):
    page_refs = refs[:2 * blocks_per_step]
    q_ref, o_ref, gate_ref = refs[2 * blocks_per_step:]
    n = pl.program_id(1)
    lane = lax.broadcasted_iota(I32, gate_ref.shape, 1)

    @pl.when(n == 0)
    def _():
        gate_ref[...] = jnp.full(gate_ref.shape, -jnp.inf, F32)

    gates = gate_ref[...]
    for i in range(blocks_per_step):
        kmean = (jnp.sum(page_refs[2 * i][...], axis=0)
                 + jnp.sum(page_refs[2 * i + 1][...], axis=0)) / float(MOBA_BLOCK)
        g = jnp.sum(kmean * q_ref[...], axis=-1, keepdims=True)
        gates = jnp.where(lane == n * blocks_per_step + i, g, gates)
    gate_ref[...] = gates

    @pl.when(n == nfull // blocks_per_step - 1)
    def _():
        gate = gate_ref[...]
        lanef = lane.astype(F32)
        picks = jnp.zeros(gate.shape, F32)
        for r in range(topk):
            best = jnp.max(gate, axis=-1, keepdims=True)
            first = jnp.min(jnp.where(gate == best, lanef, float(LANES)), axis=-1, keepdims=True)
            picks = jnp.where(lane == r, first, picks)
            gate = jnp.where(lanef == first, -jnp.inf, gate)
        o_ref[...] = picks.astype(I32)


def _moba_sample_select(pool_k, q3, page_table_flat, nfull):
    bsz, heads, _ = q3.shape
    n_pages = page_table_flat.shape[0] // bsz
    topk = min(MOBA_TOPK, nfull)
    assert nfull <= LANES and MOBA_BLOCK == 2 * PAGE_SIZE
    bps = 2 if nfull % 2 == 0 else 1
    body = functools.partial(_moba_gate_body, nfull=nfull, topk=topk, blocks_per_step=bps)
    page = (None, PAGE_SIZE, heads, HEAD_DIM)
    grid_spec = pltpu.PrefetchScalarGridSpec(
        num_scalar_prefetch=1,
        grid=(bsz, nfull // bps),
        in_specs=[pl.BlockSpec(page, lambda b, n, pt, j=j: (pt[b * n_pages + 2 * bps * n + j], 0, 0, 0))
                  for j in range(2 * bps)] + [
            pl.BlockSpec((None, heads, HEAD_DIM), lambda b, n, pt: (b, 0, 0)),
        ],
        out_specs=pl.BlockSpec((None, heads, LANES), lambda b, n, pt: (b, 0, 0)),
        scratch_shapes=[pltpu.VMEM((heads, LANES), F32)],
    )
    return pl.pallas_call(
        body,
        grid_spec=grid_spec,
        out_shape=jax.ShapeDtypeStruct((bsz, heads, LANES), I32),
        compiler_params=_cp("parallel", "arbitrary"),
        name="moba_sample_select",
    )(page_table_flat, *([pool_k] * (2 * bps)), q3)


def _moba_sample_attn_body(pt_ref, sel_ref, q_ref, kn_ref, vn_ref, kp_ref, vp_ref, o_ref,
                           m_ref, l_ref, acc_ref, *, nsteps, heads):
    h = pl.program_id(1)
    r = pl.program_id(2)
    scale = HEAD_DIM ** -0.5
    q8 = jnp.broadcast_to(q_ref[...], (8, HEAD_DIM))

    @pl.when(r == 0)
    def _():
        s_own = jnp.sum(q8 * kn_ref[...], axis=-1, keepdims=True) * scale
        m_ref[...] = jnp.broadcast_to(s_own, m_ref.shape)
        l_ref[...] = jnp.ones(l_ref.shape, F32)
        acc_ref[...] = jnp.broadcast_to(vn_ref[...], acc_ref.shape)

    group = kp_ref.shape[1]
    kp = kp_ref[...].reshape(PAGE_SIZE * group, HEAD_DIM)
    vp = vp_ref[...].reshape(PAGE_SIZE * group, HEAD_DIM)
    s = _dot_nt(q8.astype(BF16), kp.astype(BF16)) * scale
    own = lax.broadcasted_iota(I32, s.shape, 1) % group == h % group
    s = jnp.where(own, s, NEG_INF)
    m = m_ref[:, 0:1]
    m_new = jnp.maximum(m, jnp.max(s, axis=-1, keepdims=True))
    a = jnp.exp(m - m_new)
    p = jnp.exp(s - m_new)
    l_new = a * l_ref[:, 0:1] + jnp.sum(p, axis=-1, keepdims=True)
    acc_new = a * acc_ref[...] + _dot(p.astype(BF16), vp.astype(BF16))
    m_ref[...] = jnp.broadcast_to(m_new, m_ref.shape)
    l_ref[...] = jnp.broadcast_to(l_new, l_ref.shape)
    acc_ref[...] = acc_new

    @pl.when(r == nsteps - 1)
    def _():
        o_ref[...] = (acc_new / l_new)[0:1, :]


def _moba_sample_attn(qkv3, pool_k, pool_v, page_table_flat, sel_flat, heads, topk):
    bsz = qkv3.shape[0]
    n_pages = page_table_flat.shape[0] // bsz
    nsteps = topk * 2
    group = SUBLANES if heads % SUBLANES == 0 else heads

    def page(b, h, r, pt, sel):
        blk = sel[(b * heads + h) * topk + r // 2]
        return pt[b * n_pages + 2 * blk + r % 2], 0, h // group, 0

    grid_spec = pltpu.PrefetchScalarGridSpec(
        num_scalar_prefetch=2,
        grid=(bsz, heads, nsteps),
        in_specs=[
            pl.BlockSpec((None, 1, HEAD_DIM), lambda b, h, r, pt, sel: (b, 0, h)),
            pl.BlockSpec((None, 1, HEAD_DIM), lambda b, h, r, pt, sel: (b, 0, heads + h)),
            pl.BlockSpec((None, 1, HEAD_DIM), lambda b, h, r, pt, sel: (b, 0, 2 * heads + h)),
            pl.BlockSpec((None, PAGE_SIZE, group, HEAD_DIM), page),
            pl.BlockSpec((None, PAGE_SIZE, group, HEAD_DIM), page),
        ],
        out_specs=pl.BlockSpec((None, 1, HEAD_DIM), lambda b, h, r, pt, sel: (b, 0, h)),
        scratch_shapes=[pltpu.VMEM((8, LANES), F32), pltpu.VMEM((8, LANES), F32),
                        pltpu.VMEM((8, HEAD_DIM), F32)],
    )
    return pl.pallas_call(
        functools.partial(_moba_sample_attn_body, nsteps=nsteps, heads=heads),
        grid_spec=grid_spec,
        out_shape=jax.ShapeDtypeStruct((bsz, 1, heads * HEAD_DIM), F32),
        compiler_params=_cp("parallel", "parallel", "arbitrary"),
        name="moba_sample_attn",
    )(page_table_flat, sel_flat, qkv3, qkv3, qkv3, pool_k, pool_v)


def _pool_body(h_ref, prev_ref, w_ref, sc_ref, x_ref, o_ref, *, tm, tiles_per_seq, group):
    tile = pl.program_id(0) % tiles_per_seq
    h = h_ref[...]
    prev = jnp.where(tile == 0, 0.0, prev_ref[...])
    ext = jnp.concatenate([prev, h], axis=0)
    pos = tile * tm + lax.broadcasted_iota(I32, (tm, 1), 0)
    for g, w in enumerate(POOL_WINDOWS):
        sl = slice(g * group, (g + 1) * group)
        cur = ext[:, sl]
        shift = 1
        while shift < w:
            cur = cur + pltpu.roll(cur, shift, 0)
            shift *= 2
        cnt = jnp.minimum(pos + 1, w).astype(F32)
        pooled = cur[POOL_HALO:, :] / cnt - h[:, sl]
        y = _dot(pooled.astype(BF16), w_ref[g].astype(BF16))
        o_ref[:, sl] = x_ref[:, sl] + y * sc_ref[:, sl]


def _pool_mix(h, x, w_pool, scale, seq, *, tm):
    m, d = h.shape
    group = d // len(POOL_WINDOWS)
    tiles_per_seq = seq // tm
    halo_blocks = tm // POOL_HALO
    body = functools.partial(_pool_body, tm=tm, tiles_per_seq=tiles_per_seq, group=group)
    return pl.pallas_call(
        body,
        grid=(m // tm,),
        in_specs=[
            pl.BlockSpec((tm, d), lambda i: (i, 0)),
            pl.BlockSpec((POOL_HALO, d), lambda i: (jnp.maximum(i * halo_blocks - 1, 0), 0)),
            pl.BlockSpec((len(POOL_WINDOWS), group, group), lambda i: (0, 0, 0)),
            pl.BlockSpec((1, d), lambda i: (0, 0)),
            pl.BlockSpec((tm, d), lambda i: (i, 0)),
        ],
        out_specs=pl.BlockSpec((tm, d), lambda i: (i, 0)),
        out_shape=jax.ShapeDtypeStruct((m, d), F32),
        compiler_params=_cp("parallel"),
        name="pool_mix",
    )(h, h, w_pool, scale, x)


def _float_key(score):
    bits = lax.bitcast_convert_type(score, I32)
    return jnp.where(bits < 0, bits ^ 0x7FFFFFFF, bits)


def _count(mask, axes):
    ones = jnp.where(mask, 1.0, 0.0)
    for ax in (axes if isinstance(axes, tuple) else (axes,)):
        ones = jnp.sum(ones, axis=ax, keepdims=True)
    return ones


def _kth_largest_key(count_ge, k, shape):
    t = jnp.where(count_ge(jnp.zeros(shape, I32)) >= k, 0, INT_MIN).astype(I32)

    def body(i, t):
        cand = t | jnp.left_shift(jnp.int32(1), 30 - i)
        return jnp.where(count_ge(cand) >= k, cand, t)

    return lax.fori_loop(0, 31, body, t)


def _tie_limit(count_eq_below, need, shape, nbits):
    def body(i, j):
        cand = j | jnp.left_shift(jnp.int32(1), nbits - 1 - i)
        return jnp.where(count_eq_below(cand) <= need, cand, j)

    return lax.fori_loop(0, nbits, body, jnp.zeros(shape, I32))


def _dsa_prompt_body(q_ref, k_ref, v_ref, qi_ref, wq_ref, ki_ref, o_ref,
                     qs_ref, os_ref, bias_ref, lim_ref, *, seq, tq, ntop, heads, n_ranges):
    qt = pl.program_id(1)
    scale = HEAD_DIM ** -0.5
    group = heads // C_KV_HEADS
    nt = seq // tq

    qi = qi_ref[...]
    wi = wq_ref[:, IDX_DIM:IDX_DIM + IDX_HEADS] * (IDX_HEADS ** -0.5 * IDX_DIM ** -0.5)
    for h in range(heads):
        qs_ref[h] = (q_ref[:, h * HEAD_DIM:(h + 1) * HEAD_DIM] * scale).astype(BF16)

    def attend(nk):
        score = jnp.zeros((tq, nk), F32)
        for h in range(IDX_HEADS):
            d = _dot_nt(qi[:, h * IDX_DIM:(h + 1) * IDX_DIM].astype(BF16), ki_ref[0:nk, :])
            score = score + jnp.maximum(d, 0.0) * wi[:, h:h + 1]
        kpos = lax.broadcasted_iota(I32, (tq, nk), 1)
        tpos = qt * tq + lax.broadcasted_iota(I32, (tq, nk), 0)
        causal = kpos <= tpos
        key = _float_key(jnp.where(causal, score, NEG_INF))

        thr = _kth_largest_key(lambda c: _count(key >= c, -1), float(ntop), (tq, 1))
        above = _count(key > thr, -1)
        tied = key == thr
        need = float(ntop) - above
        lim_ref[...] = jnp.full((tq, 1), nk, I32)

        @pl.when(jnp.max(_count(tied, -1) - need) > 0.0)
        def _():
            lim_ref[...] = _tie_limit(lambda j: _count(tied & (kpos < j), -1), need, (tq, 1),
                                      nk.bit_length())

        chosen = (key > thr) | (tied & (kpos < lim_ref[...]))
        bias_ref[:, 0:nk] = jnp.where(chosen & causal, 0.0, NEG_INF)

        for g in range(C_KV_HEADS):
            sl = slice(g * HEAD_DIM, (g + 1) * HEAD_DIM)

            def head(j, carry, g=g, sl=sl):
                h = g * group + j
                s = _dot_nt(qs_ref[h], k_ref[0:nk, sl]) + bias_ref[:, 0:nk]
                p = jnp.exp(s - jnp.max(s, axis=-1, keepdims=True))
                l = jnp.sum(p, axis=-1, keepdims=True)
                os_ref[h] = _dot(p.astype(BF16), v_ref[0:nk, sl]) / l
                return carry

            lax.fori_loop(0, group, head, 0, unroll=2 if group % 2 == 0 else 1)

    for v in range(n_ranges):
        lo, hi = v * nt // n_ranges, (v + 1) * nt // n_ranges

        @pl.when((qt >= lo) & (qt < hi))
        def _(hi=hi):
            attend(hi * tq)

    for h in range(heads):
        o_ref[:, h * HEAD_DIM:(h + 1) * HEAD_DIM] = os_ref[h]


def _dsa_prompt_attn(proj, tail, k_bf, v_bf, ki_bf, batch, seq, heads):
    tq = DSA_Q_TILE
    nt = seq // tq
    ntop = min(DSA_TOPK, seq // 4)
    qw = heads * HEAD_DIM
    kvw = C_KV_HEADS * HEAD_DIM
    iw = IDX_HEADS * IDX_DIM
    assert (qw + 2 * kvw) % iw == 0
    n_ranges = 4 if nt % 4 == 0 else 1
    body = functools.partial(_dsa_prompt_body, seq=seq, tq=tq, ntop=ntop, heads=heads, n_ranges=n_ranges)
    return pl.pallas_call(
        body,
        grid=(batch, nt),
        in_specs=[
            pl.BlockSpec((tq, qw), lambda b, t: (b * nt + t, 0)),
            pl.BlockSpec((seq, kvw), lambda b, t: (b, 0)),
            pl.BlockSpec((seq, kvw), lambda b, t: (b, 0)),
            pl.BlockSpec((tq, iw), lambda b, t: (b * nt + t, (qw + 2 * kvw) // iw)),
            pl.BlockSpec((tq, tail.shape[1]), lambda b, t: (b * nt + t, 0)),
            pl.BlockSpec((seq, IDX_DIM), lambda b, t: (b, 0)),
        ],
        out_specs=pl.BlockSpec((tq, qw), lambda b, t: (b * nt + t, 0)),
        out_shape=jax.ShapeDtypeStruct((batch * seq, qw), F32),
        scratch_shapes=[pltpu.VMEM((heads, tq, HEAD_DIM), BF16), pltpu.VMEM((heads, tq, HEAD_DIM), F32),
                        pltpu.VMEM((tq, seq), F32), pltpu.VMEM((tq, 1), I32)],
        compiler_params=_cp("parallel", "arbitrary"),
        name="dsa_prompt",
    )(proj, k_bf, v_bf, proj, tail, ki_bf)


def _index_score_rows(qi, wi, ki):
    d = _dot_nt(qi.astype(BF16), ki.astype(BF16)) * (IDX_DIM ** -0.5)
    return jnp.sum(jnp.maximum(d, 0.0) * (wi * (IDX_HEADS ** -0.5)), axis=0, keepdims=True)


def _dsa_sample_scores_body(pt_ref, *refs, bsz):
    ki_refs = refs[:bsz]
    qi_ref, wi_ref, o_ref = refs[bsz:]
    p = pl.program_id(0)
    for b in range(bsz):
        o_ref[b, pl.ds(p, 1), :] = _index_score_rows(qi_ref[b], wi_ref[b], ki_refs[b][...])


def _dsa_sample_scores(pool_ki, qi3, wi3, page_table_flat):
    bsz = qi3.shape[0]
    n_pages = page_table_flat.shape[0] // bsz
    page_specs = [
        pl.BlockSpec((None, PAGE_SIZE, IDX_DIM), lambda p, pt, b=b: (pt[b * n_pages + p], 0, 0))
        for b in range(bsz)
    ]
    grid_spec = pltpu.PrefetchScalarGridSpec(
        num_scalar_prefetch=1,
        grid=(n_pages,),
        in_specs=page_specs + [
            pl.BlockSpec(qi3.shape, lambda p, pt: (0, 0, 0)),
            pl.BlockSpec(wi3.shape, lambda p, pt: (0, 0, 0)),
        ],
        out_specs=pl.BlockSpec((bsz, n_pages, PAGE_SIZE), lambda p, pt: (0, 0, 0)),
    )
    return pl.pallas_call(
        functools.partial(_dsa_sample_scores_body, bsz=bsz),
        grid_spec=grid_spec,
        out_shape=jax.ShapeDtypeStruct((bsz, n_pages, PAGE_SIZE), F32),
        compiler_params=_cp("arbitrary"),
        name="dsa_sample_scores",
    )(page_table_flat, *([pool_ki] * bsz), qi3, wi3)


def _dsa_sample_attn_body(pt_ref, sc_ref, qi_ref, wi_ref, kin_ref, q_ref, kn_ref, vn_ref, *refs,
                          n_pages, ntop, heads, pages_per_step):
    kp_refs = refs[:pages_per_step]
    vp_refs = refs[pages_per_step:2 * pages_per_step]
    o_ref, mask_ref, new_ref, m_ref, l_ref, acc_ref = refs[2 * pages_per_step:]
    p = pl.program_id(1)
    scale = HEAD_DIM ** -0.5
    group = heads // C_KV_HEADS
    past = n_pages * PAGE_SIZE

    @pl.when(p == 0)
    def _():
        key = _float_key(sc_ref[...])
        d_new = jnp.sum(qi_ref[...] * kin_ref[...], axis=-1, keepdims=True) * (IDX_DIM ** -0.5)
        key_new = _float_key(jnp.sum(jnp.maximum(d_new, 0.0) * (wi_ref[...] * (IDX_HEADS ** -0.5)),
                                     axis=0, keepdims=True))
        pos = (lax.broadcasted_iota(I32, key.shape, 0) * PAGE_SIZE
               + lax.broadcasted_iota(I32, key.shape, 1))

        def count_ge(c):
            return _count(key >= c, (0, 1)) + jnp.where(key_new >= c, 1.0, 0.0)

        thr = _kth_largest_key(count_ge, float(ntop), (1, 1))
        above = _count(key > thr, (0, 1)) + jnp.where(key_new > thr, 1.0, 0.0)
        tied = key == thr
        tied_new = key_new == thr
        need = float(ntop) - above

        def count_eq_below(j):
            return _count(tied & (pos < j), (0, 1)) + jnp.where(tied_new & (past < j), 1.0, 0.0)

        lim = _tie_limit(count_eq_below, need, (1, 1), (past + 1).bit_length())
        mask_ref[...] = jnp.where((key > thr) | (tied & (pos < lim)), 1.0, 0.0)
        new_ref[...] = jnp.broadcast_to(
            jnp.where((key_new > thr) | (tied_new & (past < lim)), 1.0, 0.0), new_ref.shape)
        m_ref[...] = jnp.full(m_ref.shape, NEG_INF, F32)
        l_ref[...] = jnp.zeros(l_ref.shape, F32)
        acc_ref[...] = jnp.zeros(acc_ref.shape, F32)

    q = q_ref[...]
    qb = q.astype(BF16)
    kv_head = lax.broadcasted_iota(I32, (heads, 1), 0) // group

    def per_kv_head(f):
        out = f(0)
        for g in range(1, C_KV_HEADS):
            out = jnp.where(kv_head == g, f(g), out)
        return out

    def page_scores(j):
        s_j = per_kv_head(lambda g: _dot_nt(qb, kp_refs[j][:, g, :].astype(BF16))) * scale
        return jnp.where(mask_ref[pl.ds(p * pages_per_step + j, 1), :] > 0.0, s_j, NEG_INF)

    s = jnp.concatenate([page_scores(j) for j in range(pages_per_step)], axis=1)
    m = m_ref[...]
    m_new = jnp.maximum(m, jnp.max(s, axis=-1, keepdims=True))
    a = jnp.exp(m - m_new)
    pr = jnp.where(s > NEG_INF, jnp.exp(s - m_new), 0.0)
    prb = pr.astype(BF16)
    pv = jnp.zeros(acc_ref.shape, F32)
    for j in range(pages_per_step):
        pr_j = prb[:, j * PAGE_SIZE:(j + 1) * PAGE_SIZE]
        pv = pv + per_kv_head(lambda g: _dot(pr_j, vp_refs[j][:, g, :].astype(BF16)))
    l_new = a * l_ref[...] + jnp.sum(pr, axis=-1, keepdims=True)
    acc_new = a * acc_ref[...] + pv
    m_ref[...] = m_new
    l_ref[...] = l_new
    acc_ref[...] = acc_new

    @pl.when(p == n_pages // pages_per_step - 1)
    def _():
        s_new = jnp.sum(q * kn_ref[...], axis=-1, keepdims=True) * scale
        take = new_ref[:, 0:1] > 0.0
        s_new = jnp.where(take, s_new, NEG_INF)
        m_fin = jnp.maximum(m_new, s_new)
        a2 = jnp.exp(m_new - m_fin)
        p_new = jnp.where(take, jnp.exp(s_new - m_fin), 0.0)
        l_fin = a2 * l_new + p_new
        o_ref[...] = (a2 * acc_new + p_new * vn_ref[...]) / l_fin


def _dsa_sample_attn(scores, qi3, wi3, kin3, q3, kn3, vn3, pool_k, pool_v, page_table_flat, heads):
    bsz, n_pages, _ = scores.shape
    ntop = min(DSA_TOPK, (n_pages * PAGE_SIZE + 1) // 4)
    pps = 8 if n_pages % 8 == 0 else 1
    per_b = lambda shape: pl.BlockSpec((None,) + shape, lambda b, p, pt: (b, 0, 0))
    pages = [pl.BlockSpec((None, PAGE_SIZE, C_KV_HEADS, HEAD_DIM),
                          lambda b, p, pt, j=j: (pt[b * n_pages + p * pps + j], 0, 0, 0))
             for j in range(pps)]
    grid_spec = pltpu.PrefetchScalarGridSpec(
        num_scalar_prefetch=1,
        grid=(bsz, n_pages // pps),
        in_specs=[per_b((n_pages, PAGE_SIZE)), per_b((IDX_HEADS, IDX_DIM)), per_b((IDX_HEADS, 1)),
                  per_b((1, IDX_DIM)), per_b((heads, HEAD_DIM)), per_b((heads, HEAD_DIM)),
                  per_b((heads, HEAD_DIM))] + pages + pages,
        out_specs=per_b((heads, HEAD_DIM)),
        scratch_shapes=[pltpu.VMEM((n_pages, PAGE_SIZE), F32), pltpu.VMEM((heads, LANES), F32),
                        pltpu.VMEM((heads, 1), F32), pltpu.VMEM((heads, 1), F32),
                        pltpu.VMEM((heads, HEAD_DIM), F32)],
    )
    body = functools.partial(_dsa_sample_attn_body, n_pages=n_pages, ntop=ntop, heads=heads,
                             pages_per_step=pps)
    return pl.pallas_call(
        body,
        grid_spec=grid_spec,
        out_shape=jax.ShapeDtypeStruct((bsz, heads, HEAD_DIM), F32),
        compiler_params=_cp("parallel", "arbitrary"),
        name="dsa_sample_attn",
    )(page_table_flat, scores, qi3, wi3, kin3, q3, kn3, vn3, *([pool_k] * pps), *([pool_v] * pps))


def _gla_body(q_ref, k_ref, v_ref, r_ref, low_ref, wg2_ref, bg_ref, gn_ref, s0_ref, o_ref, sf_ref,
              state_ref, *, chunk, n_valid, dk, dv):
    c = pl.program_id(2)

    @pl.when(c == 0)
    def _():
        state_ref[...] = s0_ref[...]

    x = _dot_hi(low_ref[...], wg2_ref[...]) + bg_ref[...]
    g = (jnp.minimum(x, 0.0) - jnp.log(1.0 + jnp.exp(-jnp.abs(x)))) / GLA_TAU
    row = lax.broadcasted_iota(I32, (chunk, 1), 0)
    if n_valid < chunk:
        g = jnp.where(row < n_valid, g, 0.0)
    tri = (lax.broadcasted_iota(I32, (chunk, chunk), 1)
           <= lax.broadcasted_iota(I32, (chunk, chunk), 0)).astype(BF16)
    bcum = _dot_exact_lhs(tri, g)
    q = q_ref[...] * (dk ** -0.5)
    k = k_ref[...]
    vb = v_ref[...].astype(BF16)
    sub = 8
    lane = lax.broadcasted_iota(I32, (sub, chunk), 1)
    row_groups = [jnp.zeros((sub, chunk), F32) for _ in range(chunk // sub)]
    for s in range(chunk):
        r0 = (s // sub) * sub
        rows = r0 + lax.broadcasted_iota(I32, (chunk - r0, 1), 0)
        decay = jnp.exp(jnp.where(rows >= s, bcum[r0:, :] - bcum[s:s + 1, :], -jnp.inf))
        col = jnp.sum(q[r0:, :] * k[s:s + 1, :] * decay, axis=-1, keepdims=True)
        for rg in range(s // sub, chunk // sub):
            part = col[rg * sub - r0:(rg + 1) * sub - r0, :]
            row_groups[rg] = jnp.where(lane == s, part, row_groups[rg])
    att = jnp.concatenate(row_groups, axis=0)
    state = state_ref[...]
    o = _dot(att.astype(BF16), vb) + _dot((q * jnp.exp(bcum)).astype(BF16), state.astype(BF16))

    b_last = bcum[chunk - 1:chunk, :]
    last_row = (lax.broadcasted_iota(I32, (chunk, dv), 0) == chunk - 1).astype(BF16)
    b_last_cols = _dot_exact_rhs(bcum, last_row, _dot_tn)
    kd = (k * jnp.exp(b_last - bcum)).astype(BF16)
    new_state = jnp.exp(b_last_cols) * state + _dot_tn(kd, vb)
    state_ref[...] = new_state

    @pl.when(c == pl.num_programs(2) - 1)
    def _():
        sf_ref[...] = new_state

    o_ref[...] = _rms(o) * gn_ref[...] * _silu(r_ref[...])


def _gla(proj3, low3, w_g2, b_g, g_n, s0, *, chunk, n_valid):
    bsz, length, _ = proj3.shape
    heads = GLA_HEADS
    dk, dv = s0.shape[2], s0.shape[3]
    assert dv == 2 * dk
    body = functools.partial(_gla_body, chunk=chunk, n_valid=n_valid, dk=dk, dv=dv)
    return pl.pallas_call(
        body,
        grid=(bsz, heads, length // chunk),
        in_specs=[
            pl.BlockSpec((None, chunk, dk), lambda b, h, c: (b, c, h)),
            pl.BlockSpec((None, chunk, dk), lambda b, h, c: (b, c, heads + h)),
            pl.BlockSpec((None, chunk, dv), lambda b, h, c: (b, c, heads + h)),
            pl.BlockSpec((None, chunk, dv), lambda b, h, c: (b, c, 2 * heads + h)),
            pl.BlockSpec((None, chunk, low3.shape[2]), lambda b, h, c: (b, c, 0)),
            pl.BlockSpec((low3.shape[2], dk), lambda b, h, c: (0, h)),
            pl.BlockSpec((1, dk), lambda b, h, c: (0, h)),
            pl.BlockSpec((1, dv), lambda b, h, c: (0, 0)),
            pl.BlockSpec((None, None, dk, dv), lambda b, h, c: (b, h, 0, 0)),
        ],
        out_specs=[
            pl.BlockSpec((None, chunk, dv), lambda b, h, c: (b, c, h)),
            pl.BlockSpec((None, None, dk, dv), lambda b, h, c: (b, h, 0, 0)),
        ],
        out_shape=[jax.ShapeDtypeStruct((bsz, length, heads * dv), F32),
                   jax.ShapeDtypeStruct(s0.shape, F32)],
        scratch_shapes=[pltpu.VMEM((dk, dv), F32)],
        compiler_params=_cp("parallel", "parallel", "arbitrary"),
        name="gla",
    )(proj3, proj3, proj3, proj3, low3, w_g2, b_g, g_n, s0)


def _mixer_a(xp, xs, batch, seq, norm_mix, layer, cache_k, cache_v, page_table, w_qkv, w_o, g_q, g_k):
    d = xp.shape[1]
    heads = d // HEAD_DIM
    gain = jnp.concatenate([jnp.tile(g_q, heads), jnp.tile(g_k, heads), jnp.ones((d,), F32)])[None, :]

    def qkv(x):
        return _proj(x, w_qkv, norm_gain=norm_mix, norm_layer=layer, head_gain=gain, n_head_norm_cols=2 * d)

    qkv_p = qkv(xp)
    o_p = _moba_prompt_attn(qkv_p, batch, seq, heads)
    xp = _proj(o_p, w_o, residual=xp)

    bsz = xs.shape[0]
    n_pages = page_table.shape[1]
    past = n_pages * PAGE_SIZE
    assert past % MOBA_BLOCK == 0
    nfull = past // MOBA_BLOCK
    topk = min(MOBA_TOPK, nfull)
    assert topk > 0
    qkv_s = qkv(xs)
    qkv_s3 = qkv_s[:, None, :]
    pt_flat = page_table.reshape(-1)
    sel = _moba_sample_select(cache_k, qkv_s[:, :d].reshape(bsz, heads, HEAD_DIM), pt_flat, nfull)
    sel_flat = sel[:, :, :topk].reshape(-1)
    o_s = _moba_sample_attn(qkv_s3, cache_k, cache_v, pt_flat, sel_flat, heads, topk)
    xs = _proj(o_s[:, 0, :], w_o, residual=xs)

    kv_shape = lambda t, n: t.reshape(n, -1, heads, HEAD_DIM)
    outs = (kv_shape(qkv_p[:, d:2 * d], batch), kv_shape(qkv_p[:, 2 * d:], batch),
            kv_shape(qkv_s[:, d:2 * d], bsz), kv_shape(qkv_s[:, 2 * d:], bsz))
    return xp, xs, outs


def _mixer_b(xp, xs, batch, seq, norm_mix, layer, state_pool, w_pool, scale):
    d = xp.shape[1]
    keep = state_pool.shape[1]
    scale = scale[None, :]
    hp = _rmsnorm(xp, norm_mix, layer)
    xp = _pool_mix(hp, xp, w_pool, scale, seq, tm=512)
    pool_p = hp.reshape(batch, seq, d)[:, seq - keep:]

    bsz = xs.shape[0]
    hs = _rmsnorm(xs, norm_mix, layer)
    ext = jnp.concatenate([state_pool, hs[:, None, :]], axis=1)
    assert keep + 1 == POOL_HALO
    x_ext = jnp.concatenate([jnp.zeros_like(state_pool), xs[:, None, :]], axis=1)
    y = _pool_mix(ext.reshape(bsz * POOL_HALO, d), x_ext.reshape(bsz * POOL_HALO, d), w_pool, scale,
                  POOL_HALO, tm=POOL_HALO)
    xs = y.reshape(bsz, POOL_HALO, d)[:, -1]
    return xp, xs, (pool_p, ext[:, 1:])


def _mixer_c(xp, xs, batch, seq, norm_mix, layer, cache_k, cache_v, cache_ki, page_table,
             w_in, w_o, g_q, g_k):
    d = xp.shape[1]
    heads = d // HEAD_DIM
    kvw = C_KV_HEADS * HEAD_DIM
    iw = IDX_HEADS * IDX_DIM
    main = d + 2 * kvw + iw
    w_main, w_tail = w_in[:, :main], w_in[:, main:]
    gain = jnp.concatenate([jnp.tile(g_q, heads), jnp.tile(g_k, C_KV_HEADS),
                            jnp.ones((main - d - kvw,), F32)])[None, :]

    def project(x):
        pm = _proj(x, w_main, norm_gain=norm_mix, norm_layer=layer, head_gain=gain, n_head_norm_cols=d + kvw)
        pt = _proj(x, w_tail, norm_gain=norm_mix, norm_layer=layer)
        return pm, pt

    pm, pt = project(xp)
    k_p, v_p, ki_p = pm[:, d:d + kvw], pm[:, d + kvw:d + 2 * kvw], pt[:, :IDX_DIM]
    o_p = _dsa_prompt_attn(pm, pt, k_p.astype(BF16), v_p.astype(BF16), ki_p.astype(BF16), batch, seq, heads)
    xp = _proj(o_p, w_o, residual=xp)

    bsz = xs.shape[0]
    group = heads // C_KV_HEADS
    sm, st = project(xs)
    k_s, v_s, ki_s = sm[:, d:d + kvw], sm[:, d + kvw:d + 2 * kvw], st[:, :IDX_DIM]
    qi3 = sm[:, d + 2 * kvw:].reshape(bsz, IDX_HEADS, IDX_DIM)
    wi3 = st[:, IDX_DIM:IDX_DIM + IDX_HEADS].reshape(bsz, IDX_HEADS, 1)
    pt_flat = page_table.reshape(-1)
    scores = _dsa_sample_scores(cache_ki, qi3, wi3, pt_flat)
    per_q_head = lambda t: jnp.repeat(t.reshape(bsz, C_KV_HEADS, HEAD_DIM), group, axis=1)
    o_s = _dsa_sample_attn(scores, qi3, wi3, ki_s[:, None, :], sm[:, :d].reshape(bsz, heads, HEAD_DIM),
                           per_q_head(k_s), per_q_head(v_s), cache_k, cache_v, pt_flat, heads)
    xs = _proj(o_s.reshape(bsz, d), w_o, residual=xs)

    kv4 = lambda t, n: t.reshape(n, -1, C_KV_HEADS, HEAD_DIM)
    outs = (kv4(k_p, batch), kv4(v_p, batch), ki_p.reshape(batch, seq, IDX_DIM),
            kv4(k_s, bsz), kv4(v_s, bsz), ki_s.reshape(bsz, 1, IDX_DIM))
    return xp, xs, outs


def _mixer_d(xp, xs, batch, seq, norm_mix, layer, state, w_in, w_g2, b_g, g_n, w_o):
    d = xp.shape[1]
    dk, dv = state.shape[2], state.shape[3]
    main = 2 * GLA_HEADS * dk + 2 * GLA_HEADS * dv
    w_main, w_tail = w_in[:, :main], w_in[:, main:]

    def project(x):
        pm = _proj(x, w_main, norm_gain=norm_mix, norm_layer=layer)
        pt = _proj(x, w_tail, norm_gain=norm_mix, norm_layer=layer)
        return pm, pt

    pm, pt = project(xp)
    zero_state = jnp.zeros((batch,) + state.shape[1:], F32)
    o_p, s_p = _gla(pm.reshape(batch, seq, main), pt.reshape(batch, seq, -1), w_g2, b_g[None, :],
                    g_n[None, :], zero_state, chunk=min(GLA_CHUNK, seq), n_valid=min(GLA_CHUNK, seq))
    xp = _proj(o_p.reshape(batch * seq, GLA_HEADS * dv), w_o, residual=xp)

    bsz = xs.shape[0]
    sm, st = project(xs)
    pad = lambda t: jnp.pad(t[:, None, :], ((0, 0), (0, GLA_SAMPLE_ROWS - 1), (0, 0)))
    o_s, s_s = _gla(pad(sm), pad(st), w_g2, b_g[None, :], g_n[None, :], state,
                    chunk=GLA_SAMPLE_ROWS, n_valid=1)
    xs = _proj(o_s[:, 0, :], w_o, residual=xs)
    return xp, xs, (s_p, s_s)


def kernel(x_prompt, x_sample, cache_a_k, cache_a_v, state_b_pool, cache_c_k, cache_c_v, cache_c_idx_k,
           state_d_gla, page_table, norm_ffn1, ffn1_w_gate, ffn1_w_up, ffn1_w_down, norm_mix, norm_ffn2,
           ffn2_w_gate, ffn2_w_up, ffn2_w_down, a_w_qkv, a_w_o, a_g_q, a_g_k, b_w_pool, b_scale, c_w_in,
           c_w_o, c_g_q, c_g_k, d_w_in, d_w_g2, d_b_g, d_g_n, d_w_o):
    batch, seq, d = x_prompt.shape
    bsz = x_sample.shape[0]
    assert x_sample.shape[1] == 1
    xp = x_prompt.reshape(batch * seq, d)
    xs = x_sample.reshape(bsz, d)
    depth = norm_ffn1.shape[0]
    ffn1_w_gate, ffn1_w_up, ffn1_w_down, ffn2_w_gate, ffn2_w_up, ffn2_w_down = (
        w.astype(BF16) for w in (ffn1_w_gate, ffn1_w_up, ffn1_w_down, ffn2_w_gate, ffn2_w_up, ffn2_w_down))
    outs = {}
    for i in range(depth):
        xp = _ffn(xp, norm_ffn1, ffn1_w_gate, ffn1_w_up, ffn1_w_down, i)
        xs = _ffn(xs, norm_ffn1, ffn1_w_gate, ffn1_w_up, ffn1_w_down, i)
        m = i % 4
        if m == 0:
            xp, xs, outs["a"] = _mixer_a(xp, xs, batch, seq, norm_mix, i, cache_a_k, cache_a_v, page_table,
                                         a_w_qkv, a_w_o, a_g_q, a_g_k)
        elif m == 1:
            xp, xs, outs["b"] = _mixer_b(xp, xs, batch, seq, norm_mix, i, state_b_pool, b_w_pool, b_scale)
        elif m == 2:
            xp, xs, outs["c"] = _mixer_c(xp, xs, batch, seq, norm_mix, i, cache_c_k, cache_c_v, cache_c_idx_k,
                                         page_table, c_w_in, c_w_o, c_g_q, c_g_k)
        else:
            xp, xs, outs["d"] = _mixer_d(xp, xs, batch, seq, norm_mix, i, state_d_gla, d_w_in, d_w_g2,
                                         d_b_g, d_g_n, d_w_o)
        xp = _ffn(xp, norm_ffn2, ffn2_w_gate, ffn2_w_up, ffn2_w_down, i)
        xs = _ffn(xs, norm_ffn2, ffn2_w_gate, ffn2_w_up, ffn2_w_down, i)
    return (xp.reshape(batch, seq, d), xs.reshape(bsz, 1, d)) + outs["a"] + outs["b"] + outs["c"] + outs["d"]
```

```python
import functools

import jax
import jax.numpy as jnp
from jax import lax
from jax.experimental import pallas as pl
from jax.experimental.pallas import tpu as pltpu

F32 = jnp.float32
BF16 = jnp.bfloat16
I32 = jnp.int32

NORM_EPS = 1e-6
NEG_INF = -1e30
HEAD_DIM = 128
LANES = 128
SUBLANES = 8
PAGE_SIZE = 128
MOBA_BLOCK = 256
MOBA_TOPK = 3
POOL_WINDOWS = (2, 4, 8, 16)
POOL_HALO = 16
C_KV_HEADS = 4
IDX_HEADS = 16
IDX_DIM = 64
DSA_TOPK = 256
DSA_Q_TILE = 128
GLA_HEADS = 4
GLA_GATE_RANK = 16
GLA_TAU = 16.0
GLA_CHUNK = 64
GLA_SAMPLE_ROWS = 16
VMEM_LIMIT = 56 * 1024 * 1024

INT_MIN = -2147483648


def _cp(*sem):
    return pltpu.CompilerParams(dimension_semantics=sem, vmem_limit_bytes=VMEM_LIMIT)


ROW_TILE = 1024
COL_TILE = 512
FFN_OUT_TILE = 256


def _row_tile(m):
    return ROW_TILE if m % ROW_TILE == 0 else m


FFN_HIDDEN_TILE = 512


def _dot(a, b):
    return jnp.dot(a, b, preferred_element_type=F32)


def _dot_nt(a, b):
    return lax.dot_general(a, b, (((1,), (1,)), ((), ())), preferred_element_type=F32)


def _dot_tn(a, b):
    return lax.dot_general(a, b, (((0,), (0,)), ((), ())), preferred_element_type=F32)


def _split3(a):
    a1 = a.astype(BF16)
    r1 = a - a1.astype(F32)
    a2 = r1.astype(BF16)
    a3 = (r1 - a2.astype(F32)).astype(BF16)
    return a1, a2, a3


def _dot_hi(a, b, dot=_dot):
    a1, a2, _ = _split3(a)
    b1, b2, _ = _split3(b)
    return dot(a1, b1) + (dot(a1, b2) + dot(a2, b1))


def _dot_exact_lhs(a01, b, dot=_dot):
    b1, b2, b3 = _split3(b)
    return dot(a01, b1) + (dot(a01, b2) + dot(a01, b3))


def _dot_exact_rhs(a, b01, dot=_dot):
    a1, a2, a3 = _split3(a)
    return dot(a1, b01) + (dot(a2, b01) + dot(a3, b01))


def _rms(x):
    return x * lax.rsqrt(jnp.mean(x * x, axis=-1, keepdims=True) + NORM_EPS)


def _silu(x):
    return x * jax.nn.sigmoid(x)


def _ffn_body(x_ref, g_ref, wg_ref, wu_ref, wd_ref, res_ref, o_ref, xn_ref, h_ref, *, n_up, tf):
    j = pl.program_id(1)

    @pl.when(j == 0)
    def _():
        xn_ref[...] = (_rms(x_ref[...]) * g_ref[...]).astype(BF16)

    @pl.when(j < n_up)
    def _():
        xn = xn_ref[...]
        a = _dot(xn, wg_ref[...])
        u = _dot(xn, wu_ref[...])
        hidden = (_silu(a) * u).astype(BF16)
        for c in range(n_up):
            @pl.when(j == c)
            def _(c=c):
                h_ref[:, c * tf:(c + 1) * tf] = hidden

    @pl.when(j >= n_up)
    def _():
        o_ref[...] = res_ref[...] + 0.5 * _dot(h_ref[...], wd_ref[...])


def _ffn(x, g, wg, wu, wd, layer):
    m, d = x.shape
    f = wg.shape[-1]
    tm = _row_tile(m)
    tf = FFN_HIDDEN_TILE
    tn = FFN_OUT_TILE
    n_up = f // tf
    up = lambda j: jnp.minimum(j, n_up - 1)
    down = lambda j: jnp.maximum(j - n_up, 0)
    return pl.pallas_call(
        functools.partial(_ffn_body, n_up=n_up, tf=tf),
        grid=(m // tm, n_up + d // tn),
        in_specs=[
            pl.BlockSpec((tm, d), lambda i, j: (i, 0), pipeline_mode=pl.Buffered(1)),
            pl.BlockSpec((None, 1, d), lambda i, j: (layer, 0, 0)),
            pl.BlockSpec((None, d, tf), lambda i, j: (layer, 0, up(j))),
            pl.BlockSpec((None, d, tf), lambda i, j: (layer, 0, up(j))),
            pl.BlockSpec((None, f, tn), lambda i, j: (layer, 0, down(j))),
            pl.BlockSpec((tm, tn), lambda i, j: (i, down(j))),
        ],
        out_specs=pl.BlockSpec((tm, tn), lambda i, j: (i, down(j))),
        out_shape=jax.ShapeDtypeStruct((m, d), F32),
        scratch_shapes=[pltpu.VMEM((tm, d), BF16), pltpu.VMEM((tm, f), BF16)],
        compiler_params=_cp("parallel", "arbitrary"),
        name="ffn",
    )(x, g[:, None, :], wg, wu, wd, x)


def _proj_body(*refs, pre_norm, n_head_norm_tiles, residual, tn):
    it = iter(refs)
    x_ref = next(it)
    g_ref = next(it) if pre_norm else None
    w_ref = next(it)
    cg_ref = next(it) if n_head_norm_tiles else None
    r_ref = next(it) if residual else None
    o_ref = next(it)
    xn_ref = next(it)
    j = pl.program_id(1)

    @pl.when(j == 0)
    def _():
        x = x_ref[...]
        if pre_norm:
            x = _rms(x) * g_ref[...]
        xn_ref[...] = x.astype(BF16)

    y = _dot(xn_ref[...], w_ref[...].astype(BF16))
    if residual:
        y = r_ref[...] + y

    if n_head_norm_tiles:
        @pl.when(j < n_head_norm_tiles)
        def _():
            for c in range(tn // HEAD_DIM):
                sl = slice(c * HEAD_DIM, (c + 1) * HEAD_DIM)
                o_ref[:, sl] = _rms(y[:, sl]) * cg_ref[:, sl]

        @pl.when(j >= n_head_norm_tiles)
        def _():
            o_ref[...] = y
    else:
        o_ref[...] = y


def _proj(x, w, *, norm_gain=None, norm_layer=0, head_gain=None, n_head_norm_cols=0, residual=None):
    m, k = x.shape
    n = w.shape[1]
    tm = _row_tile(m)
    tn = COL_TILE if n % COL_TILE == 0 else n
    assert n_head_norm_cols % tn == 0
    n_head_norm_tiles = n_head_norm_cols // tn
    pre_norm = norm_gain is not None
    args = [x]
    specs = [pl.BlockSpec((tm, k), lambda i, j: (i, 0))]
    if pre_norm:
        args.append(norm_gain[:, None, :])
        specs.append(pl.BlockSpec((None, 1, k), lambda i, j: (norm_layer, 0, 0)))
    args.append(w)
    specs.append(pl.BlockSpec((k, tn), lambda i, j: (0, j)))
    if n_head_norm_tiles:
        args.append(head_gain)
        specs.append(pl.BlockSpec((1, tn), lambda i, j: (0, j)))
    if residual is not None:
        args.append(residual)
        specs.append(pl.BlockSpec((tm, tn), lambda i, j: (i, j)))
    body = functools.partial(_proj_body, pre_norm=pre_norm, n_head_norm_tiles=n_head_norm_tiles,
                             residual=residual is not None, tn=tn)
    return pl.pallas_call(
        body,
        grid=(m // tm, n // tn),
        in_specs=specs,
        out_specs=pl.BlockSpec((tm, tn), lambda i, j: (i, j)),
        out_shape=jax.ShapeDtypeStruct((m, n), F32),
        scratch_shapes=[pltpu.VMEM((tm, k), BF16)],
        compiler_params=_cp("parallel", "arbitrary"),
        name="proj",
    )(*args)


def _rmsnorm_body(x_ref, g_ref, o_ref):
    o_ref[...] = _rms(x_ref[...]) * g_ref[...]


def _rmsnorm(x, gains, layer):
    m, d = x.shape
    tm = _row_tile(m)
    return pl.pallas_call(
        _rmsnorm_body,
        grid=(m // tm,),
        in_specs=[pl.BlockSpec((tm, d), lambda i: (i, 0)),
                  pl.BlockSpec((None, 1, d), lambda i: (layer, 0, 0))],
        out_specs=pl.BlockSpec((tm, d), lambda i: (i, 0)),
        out_shape=jax.ShapeDtypeStruct((m, d), F32),
        compiler_params=_cp("parallel"),
        name="rmsnorm",
    )(x, gains[:, None, :])


def _moba_prompt_body(q_ref, k_ref, v_ref, o_ref, kmean_ref, qa_ref, ka_ref, vb_ref, *,
                      seq, nblk, blk, topk, n_ranges):
    qt = pl.program_id(2)
    scale = HEAD_DIM ** -0.5
    big = -NEG_INF

    @pl.when(qt == 0)
    def _():
        kmean_ref[...] = jnp.zeros(kmean_ref.shape, F32)
        for n in range(nblk):
            kmean_ref[n:n + 1, :] = jnp.mean(k_ref[n * blk:(n + 1) * blk, :], axis=0, keepdims=True)
        q = q_ref[...]
        gate = _dot_hi(q, kmean_ref[...], _dot_nt)
        col = lax.broadcasted_iota(I32, gate.shape, 1)
        colf = col.astype(F32)
        own = lax.broadcasted_iota(I32, gate.shape, 0) // blk
        gate = jnp.where(col < own, gate, NEG_INF)
        picked = jnp.zeros(gate.shape, F32)
        for _ in range(topk):
            best = jnp.max(gate, axis=-1, keepdims=True)
            first = jnp.min(jnp.where(gate == best, colf, float(LANES)), axis=-1, keepdims=True)
            hit = colf == first
            picked = jnp.where(hit, 1.0, picked)
            gate = jnp.where(hit, -jnp.inf, gate)
        picked = jnp.where(col < own, picked, jnp.where(col == own, 1.0, 0.0))
        qa_ref[:, 0:HEAD_DIM] = (q * scale).astype(BF16)
        qa_ref[:, HEAD_DIM:] = (picked - 1.0).astype(BF16)
        key_block = lax.broadcasted_iota(I32, (seq, LANES), 0) // blk
        ka_ref[:, 0:HEAD_DIM] = k_ref[...].astype(BF16)
        ka_ref[:, HEAD_DIM:] = jnp.where(key_block == lax.broadcasted_iota(I32, (seq, LANES), 1),
                                         big, 0.0).astype(BF16)
        vb_ref[...] = v_ref[...].astype(BF16)

    qa = qa_ref[pl.ds(pl.multiple_of(qt * blk, blk), blk), :]

    def attend(nk, tail):
        s = _dot_nt(qa, ka_ref[0:nk, :])
        kpos = nk - tail + lax.broadcasted_iota(I32, (blk, tail), 1)
        tpos = qt * blk + lax.broadcasted_iota(I32, (blk, tail), 0)
        s_tail = jnp.where(kpos <= tpos, s[:, nk - tail:], NEG_INF)
        s = s_tail if tail == nk else jnp.concatenate([s[:, :nk - tail], s_tail], axis=1)
        p = jnp.exp(s - jnp.max(s, axis=-1, keepdims=True))
        l = jnp.sum(p, axis=-1, keepdims=True)
        o_ref[...] = _dot(p.astype(BF16), vb_ref[0:nk, :]) / l

    for v in range(n_ranges):
        lo, hi = v * nblk // n_ranges, (v + 1) * nblk // n_ranges

        @pl.when((qt >= lo) & (qt < hi))
        def _(lo=lo, hi=hi):
            attend(hi * blk, (hi - lo) * blk)


def _moba_prompt_attn(qkv, batch, seq, heads):
    blk = MOBA_BLOCK
    nblk = seq // blk
    assert nblk <= LANES
    topk = min(MOBA_TOPK, (seq - 1) // blk)
    n_ranges = 4 if nblk % 4 == 0 else 1
    body = functools.partial(_moba_prompt_body, seq=seq, nblk=nblk, blk=blk, topk=topk, n_ranges=n_ranges)
    return pl.pallas_call(
        body,
        grid=(batch, heads, nblk),
        in_specs=[
            pl.BlockSpec((seq, HEAD_DIM), lambda b, h, t: (b, h)),
            pl.BlockSpec((seq, HEAD_DIM), lambda b, h, t: (b, heads + h)),
            pl.BlockSpec((seq, HEAD_DIM), lambda b, h, t: (b, 2 * heads + h)),
        ],
        out_specs=pl.BlockSpec((blk, HEAD_DIM), lambda b, h, t: (b * nblk + t, h)),
        out_shape=jax.ShapeDtypeStruct((batch * seq, heads * HEAD_DIM), F32),
        scratch_shapes=[pltpu.VMEM((LANES, HEAD_DIM), F32), pltpu.VMEM((seq, 2 * HEAD_DIM), BF16),
                        pltpu.VMEM((seq, 2 * HEAD_DIM), BF16), pltpu.VMEM((seq, HEAD_DIM), BF16)],
        compiler_params=_cp("parallel", "parallel", "arbitrary"),
        name="moba_prompt",
    )(qkv, qkv, qkv)


def _moba_gate_body(pt_ref, *refs, nfull, topk, blocks_per_step):
    page_refs = refs[:2 * blocks_per_step]
    q_ref, o_ref, gate_ref = refs[2 * blocks_per_step:]
    n = pl.program_id(1)
    lane = lax.broadcasted_iota(I32, gate_ref.shape, 1)

    @pl.when(n == 0)
    def _():
        gate_ref[...] = jnp.full(gate_ref.shape, -jnp.inf, F32)

    gates = gate_ref[...]
    for i in range(blocks_per_step):
        kmean = (jnp.sum(page_refs[2 * i][...], axis=0)
                 + jnp.sum(page_refs[2 * i + 1][...], axis=0)) / float(MOBA_BLOCK)
        g = jnp.sum(kmean * q_ref[...], axis=-1, keepdims=True)
        gates = jnp.where(lane == n * blocks_per_step + i, g, gates)
    gate_ref[...] = gates

    @pl.when(n == nfull // blocks_per_step - 1)
    def _():
        gate = gate_ref[...]
        lanef = lane.astype(F32)
        picks = jnp.zeros(gate.shape, F32)
        for r in range(topk):
            best = jnp.max(gate, axis=-1, keepdims=True)
            first = jnp.min(jnp.where(gate == best, lanef, float(LANES)), axis=-1, keepdims=True)
            picks = jnp.where(lane == r, first, picks)
            gate = jnp.where(lanef == first, -jnp.inf, gate)
        o_ref[...] = picks.astype(I32)


def _moba_sample_select(pool_k, q3, page_table_flat, nfull):
    bsz, heads, _ = q3.shape
    n_pages = page_table_flat.shape[0] // bsz
    topk = min(MOBA_TOPK, nfull)
    assert nfull <= LANES and MOBA_BLOCK == 2 * PAGE_SIZE
    bps = 2 if nfull % 2 == 0 else 1
    body = functools.partial(_moba_gate_body, nfull=nfull, topk=topk, blocks_per_step=bps)
    page = (None, PAGE_SIZE, heads, HEAD_DIM)
    grid_spec = pltpu.PrefetchScalarGridSpec(
        num_scalar_prefetch=1,
        grid=(bsz, nfull // bps),
        in_specs=[pl.BlockSpec(page, lambda b, n, pt, j=j: (pt[b * n_pages + 2 * bps * n + j], 0, 0, 0))
                  for j in range(2 * bps)] + [
            pl.BlockSpec((None, heads, HEAD_DIM), lambda b, n, pt: (b, 0, 0)),
        ],
        out_specs=pl.BlockSpec((None, heads, LANES), lambda b, n, pt: (b, 0, 0)),
        scratch_shapes=[pltpu.VMEM((heads, LANES), F32)],
    )
    return pl.pallas_call(
        body,
        grid_spec=grid_spec,
        out_shape=jax.ShapeDtypeStruct((bsz, heads, LANES), I32),
        compiler_params=_cp("parallel", "arbitrary"),
        name="moba_sample_select",
    )(page_table_flat, *([pool_k] * (2 * bps)), q3)


def _moba_sample_attn_body(pt_ref, sel_ref, q_ref, kn_ref, vn_ref, kp_ref, vp_ref, o_ref,
                           m_ref, l_ref, acc_ref, *, nsteps, heads):
    h = pl.program_id(1)
    r = pl.program_id(2)
    scale = HEAD_DIM ** -0.5
    q8 = jnp.broadcast_to(q_ref[...], (8, HEAD_DIM))

    @pl.when(r == 0)
    def _():
        s_own = jnp.sum(q8 * kn_ref[...], axis=-1, keepdims=True) * scale
        m_ref[...] = jnp.broadcast_to(s_own, m_ref.shape)
        l_ref[...] = jnp.ones(l_ref.shape, F32)
        acc_ref[...] = jnp.broadcast_to(vn_ref[...], acc_ref.shape)

    group = kp_ref.shape[1]
    kp = kp_ref[...].reshape(PAGE_SIZE * group, HEAD_DIM)
    vp = vp_ref[...].reshape(PAGE_SIZE * group, HEAD_DIM)
    s = _dot_nt(q8.astype(BF16), kp.astype(BF16)) * scale
    own = lax.broadcasted_iota(I32, s.shape, 1) % group == h % group
    s = jnp.where(own, s, NEG_INF)
    m = m_ref[:, 0:1]
    m_new = jnp.maximum(m, jnp.max(s, axis=-1, keepdims=True))
    a = jnp.exp(m - m_new)
    p = jnp.exp(s - m_new)
    l_new = a * l_ref[:, 0:1] + jnp.sum(p, axis=-1, keepdims=True)
    acc_new = a * acc_ref[...] + _dot(p.astype(BF16), vp.astype(BF16))
    m_ref[...] = jnp.broadcast_to(m_new, m_ref.shape)
    l_ref[...] = jnp.broadcast_to(l_new, l_ref.shape)
    acc_ref[...] = acc_new

    @pl.when(r == nsteps - 1)
    def _():
        o_ref[...] = (acc_new / l_new)[0:1, :]


def _moba_sample_attn(qkv3, pool_k, pool_v, page_table_flat, sel_flat, heads, topk):
    bsz = qkv3.shape[0]
    n_pages = page_table_flat.shape[0] // bsz
    nsteps = topk * 2
    group = SUBLANES if heads % SUBLANES == 0 else heads

    def page(b, h, r, pt, sel):
        blk = sel[(b * heads + h) * topk + r // 2]
        return pt[b * n_pages + 2 * blk + r % 2], 0, h // group, 0

    grid_spec = pltpu.PrefetchScalarGridSpec(
        num_scalar_prefetch=2,
        grid=(bsz, heads, nsteps),
        in_specs=[
            pl.BlockSpec((None, 1, HEAD_DIM), lambda b, h, r, pt, sel: (b, 0, h)),
            pl.BlockSpec((None, 1, HEAD_DIM), lambda b, h, r, pt, sel: (b, 0, heads + h)),
            pl.BlockSpec((None, 1, HEAD_DIM), lambda b, h, r, pt, sel: (b, 0, 2 * heads + h)),
            pl.BlockSpec((None, PAGE_SIZE, group, HEAD_DIM), page),
            pl.BlockSpec((None, PAGE_SIZE, group, HEAD_DIM), page),
        ],
        out_specs=pl.BlockSpec((None, 1, HEAD_DIM), lambda b, h, r, pt, sel: (b, 0, h)),
        scratch_shapes=[pltpu.VMEM((8, LANES), F32), pltpu.VMEM((8, LANES), F32),
                        pltpu.VMEM((8, HEAD_DIM), F32)],
    )
    return pl.pallas_call(
        functools.partial(_moba_sample_attn_body, nsteps=nsteps, heads=heads),
        grid_spec=grid_spec,
        out_shape=jax.ShapeDtypeStruct((bsz, 1, heads * HEAD_DIM), F32),
        compiler_params=_cp("parallel", "parallel", "arbitrary"),
        name="moba_sample_attn",
    )(page_table_flat, sel_flat, qkv3, qkv3, qkv3, pool_k, pool_v)


def _pool_body(h_ref, prev_ref, w_ref, sc_ref, x_ref, o_ref, *, tm, tiles_per_seq, group):
    tile = pl.program_id(0) % tiles_per_seq
    h = h_ref[...]
    prev = jnp.where(tile == 0, 0.0, prev_ref[...])
    ext = jnp.concatenate([prev, h], axis=0)
    pos = tile * tm + lax.broadcasted_iota(I32, (tm, 1), 0)
    for g, w in enumerate(POOL_WINDOWS):
        sl = slice(g * group, (g + 1) * group)
        cur = ext[:, sl]
        shift = 1
        while shift < w:
            cur = cur + pltpu.roll(cur, shift, 0)
            shift *= 2
        cnt = jnp.minimum(pos + 1, w).astype(F32)
        pooled = cur[POOL_HALO:, :] / cnt - h[:, sl]
        y = _dot(pooled.astype(BF16), w_ref[g].astype(BF16))
        o_ref[:, sl] = x_ref[:, sl] + y * sc_ref[:, sl]


def _pool_mix(h, x, w_pool, scale, seq, *, tm):
    m, d = h.shape
    group = d // len(POOL_WINDOWS)
    tiles_per_seq = seq // tm
    halo_blocks = tm // POOL_HALO
    body = functools.partial(_pool_body, tm=tm, tiles_per_seq=tiles_per_seq, group=group)
    return pl.pallas_call(
        body,
        grid=(m // tm,),
        in_specs=[
            pl.BlockSpec((tm, d), lambda i: (i, 0)),
            pl.BlockSpec((POOL_HALO, d), lambda i: (jnp.maximum(i * halo_blocks - 1, 0), 0)),
            pl.BlockSpec((len(POOL_WINDOWS), group, group), lambda i: (0, 0, 0)),
            pl.BlockSpec((1, d), lambda i: (0, 0)),
            pl.BlockSpec((tm, d), lambda i: (i, 0)),
        ],
        out_specs=pl.BlockSpec((tm, d), lambda i: (i, 0)),
        out_shape=jax.ShapeDtypeStruct((m, d), F32),
        compiler_params=_cp("parallel"),
        name="pool_mix",
    )(h, h, w_pool, scale, x)


def _float_key(score):
    bits = lax.bitcast_convert_type(score, I32)
    return jnp.where(bits < 0, bits ^ 0x7FFFFFFF, bits)


def _count(mask, axes):
    ones = jnp.where(mask, 1.0, 0.0)
    for ax in (axes if isinstance(axes, tuple) else (axes,)):
        ones = jnp.sum(ones, axis=ax, keepdims=True)
    return ones


def _kth_largest_key(count_ge, k, shape):
    t = jnp.where(count_ge(jnp.zeros(shape, I32)) >= k, 0, INT_MIN).astype(I32)

    def body(i, t):
        cand = t | jnp.left_shift(jnp.int32(1), 30 - i)
        return jnp.where(count_ge(cand) >= k, cand, t)

    return lax.fori_loop(0, 31, body, t)


def _tie_limit(count_eq_below, need, shape, nbits):
    def body(i, j):
        cand = j | jnp.left_shift(jnp.int32(1), nbits - 1 - i)
        return jnp.where(count_eq_below(cand) <= need, cand, j)

    return lax.fori_loop(0, nbits, body, jnp.zeros(shape, I32))


def _dsa_prompt_body(q_ref, k_ref, v_ref, qi_ref, wq_ref, ki_ref, o_ref,
                     qs_ref, os_ref, bias_ref, lim_ref, *, seq, tq, ntop, heads, n_ranges):
    qt = pl.program_id(1)
    scale = HEAD_DIM ** -0.5
    group = heads // C_KV_HEADS
    nt = seq // tq

    qi = qi_ref[...]
    wi = wq_ref[:, IDX_DIM:IDX_DIM + IDX_HEADS] * (IDX_HEADS ** -0.5 * IDX_DIM ** -0.5)
    for h in range(heads):
        qs_ref[h] = (q_ref[:, h * HEAD_DIM:(h + 1) * HEAD_DIM] * scale).astype(BF16)

    def attend(nk):
        score = jnp.zeros((tq, nk), F32)
        for h in range(IDX_HEADS):
            d = _dot_nt(qi[:, h * IDX_DIM:(h + 1) * IDX_DIM].astype(BF16), ki_ref[0:nk, :])
            score = score + jnp.maximum(d, 0.0) * wi[:, h:h + 1]
        kpos = lax.broadcasted_iota(I32, (tq, nk), 1)
        tpos = qt * tq + lax.broadcasted_iota(I32, (tq, nk), 0)
        causal = kpos <= tpos
        key = _float_key(jnp.where(causal, score, NEG_INF))

        thr = _kth_largest_key(lambda c: _count(key >= c, -1), float(ntop), (tq, 1))
        above = _count(key > thr, -1)
        tied = key == thr
        need = float(ntop) - above
        lim_ref[...] = jnp.full((tq, 1), nk, I32)

        @pl.when(jnp.max(_count(tied, -1) - need) > 0.0)
        def _():
            lim_ref[...] = _tie_limit(lambda j: _count(tied & (kpos < j), -1), need, (tq, 1),
                                      nk.bit_length())

        chosen = (key > thr) | (tied & (kpos < lim_ref[...]))
        bias_ref[:, 0:nk] = jnp.where(chosen & causal, 0.0, NEG_INF)

        for g in range(C_KV_HEADS):
            sl = slice(g * HEAD_DIM, (g + 1) * HEAD_DIM)

            def head(j, carry, g=g, sl=sl):
                h = g * group + j
                s = _dot_nt(qs_ref[h], k_ref[0:nk, sl]) + bias_ref[:, 0:nk]
                p = jnp.exp(s - jnp.max(s, axis=-1, keepdims=True))
                l = jnp.sum(p, axis=-1, keepdims=True)
                os_ref[h] = _dot(p.astype(BF16), v_ref[0:nk, sl]) / l
                return carry

            lax.fori_loop(0, group, head, 0)

    for v in range(n_ranges):
        lo, hi = v * nt // n_ranges, (v + 1) * nt // n_ranges

        @pl.when((qt >= lo) & (qt < hi))
        def _(hi=hi):
            attend(hi * tq)

    for h in range(heads):
        o_ref[:, h * HEAD_DIM:(h + 1) * HEAD_DIM] = os_ref[h]


def _dsa_prompt_attn(proj, tail, k_bf, v_bf, ki_bf, batch, seq, heads):
    tq = DSA_Q_TILE
    nt = seq // tq
    ntop = min(DSA_TOPK, seq // 4)
    qw = heads * HEAD_DIM
    kvw = C_KV_HEADS * HEAD_DIM
    iw = IDX_HEADS * IDX_DIM
    assert (qw + 2 * kvw) % iw == 0
    n_ranges = 4 if nt % 4 == 0 else 1
    body = functools.partial(_dsa_prompt_body, seq=seq, tq=tq, ntop=ntop, heads=heads, n_ranges=n_ranges)
    return pl.pallas_call(
        body,
        grid=(batch, nt),
        in_specs=[
            pl.BlockSpec((tq, qw), lambda b, t: (b * nt + t, 0)),
            pl.BlockSpec((seq, kvw), lambda b, t: (b, 0)),
            pl.BlockSpec((seq, kvw), lambda b, t: (b, 0)),
            pl.BlockSpec((tq, iw), lambda b, t: (b * nt + t, (qw + 2 * kvw) // iw)),
            pl.BlockSpec((tq, tail.shape[1]), lambda b, t: (b * nt + t, 0)),
            pl.BlockSpec((seq, IDX_DIM), lambda b, t: (b, 0)),
        ],
        out_specs=pl.BlockSpec((tq, qw), lambda b, t: (b * nt + t, 0)),
        out_shape=jax.ShapeDtypeStruct((batch * seq, qw), F32),
        scratch_shapes=[pltpu.VMEM((heads, tq, HEAD_DIM), BF16), pltpu.VMEM((heads, tq, HEAD_DIM), F32),
                        pltpu.VMEM((tq, seq), F32), pltpu.VMEM((tq, 1), I32)],
        compiler_params=_cp("parallel", "arbitrary"),
        name="dsa_prompt",
    )(proj, k_bf, v_bf, proj, tail, ki_bf)


def _index_score_rows(qi, wi, ki):
    d = _dot_nt(qi.astype(BF16), ki.astype(BF16)) * (IDX_DIM ** -0.5)
    return jnp.sum(jnp.maximum(d, 0.0) * (wi * (IDX_HEADS ** -0.5)), axis=0, keepdims=True)


def _dsa_sample_scores_body(pt_ref, *refs, bsz):
    ki_refs = refs[:bsz]
    qi_ref, wi_ref, o_ref = refs[bsz:]
    p = pl.program_id(0)
    for b in range(bsz):
        o_ref[b, pl.ds(p, 1), :] = _index_score_rows(qi_ref[b], wi_ref[b], ki_refs[b][...])


def _dsa_sample_scores(pool_ki, qi3, wi3, page_table_flat):
    bsz = qi3.shape[0]
    n_pages = page_table_flat.shape[0] // bsz
    page_specs = [
        pl.BlockSpec((None, PAGE_SIZE, IDX_DIM), lambda p, pt, b=b: (pt[b * n_pages + p], 0, 0))
        for b in range(bsz)
    ]
    grid_spec = pltpu.PrefetchScalarGridSpec(
        num_scalar_prefetch=1,
        grid=(n_pages,),
        in_specs=page_specs + [
            pl.BlockSpec(qi3.shape, lambda p, pt: (0, 0, 0)),
            pl.BlockSpec(wi3.shape, lambda p, pt: (0, 0, 0)),
        ],
        out_specs=pl.BlockSpec((bsz, n_pages, PAGE_SIZE), lambda p, pt: (0, 0, 0)),
    )
    return pl.pallas_call(
        functools.partial(_dsa_sample_scores_body, bsz=bsz),
        grid_spec=grid_spec,
        out_shape=jax.ShapeDtypeStruct((bsz, n_pages, PAGE_SIZE), F32),
        compiler_params=_cp("arbitrary"),
        name="dsa_sample_scores",
    )(page_table_flat, *([pool_ki] * bsz), qi3, wi3)


def _dsa_sample_attn_body(pt_ref, sc_ref, qi_ref, wi_ref, kin_ref, q_ref, kn_ref, vn_ref, *refs,
                          n_pages, ntop, heads, pages_per_step):
    kp_refs = refs[:pages_per_step]
    vp_refs = refs[pages_per_step:2 * pages_per_step]
    o_ref, mask_ref, new_ref, m_ref, l_ref, acc_ref = refs[2 * pages_per_step:]
    p = pl.program_id(1)
    scale = HEAD_DIM ** -0.5
    group = heads // C_KV_HEADS
    past = n_pages * PAGE_SIZE

    @pl.when(p == 0)
    def _():
        key = _float_key(sc_ref[...])
        d_new = jnp.sum(qi_ref[...] * kin_ref[...], axis=-1, keepdims=True) * (IDX_DIM ** -0.5)
        key_new = _float_key(jnp.sum(jnp.maximum(d_new, 0.0) * (wi_ref[...] * (IDX_HEADS ** -0.5)),
                                     axis=0, keepdims=True))
        pos = (lax.broadcasted_iota(I32, key.shape, 0) * PAGE_SIZE
               + lax.broadcasted_iota(I32, key.shape, 1))

        def count_ge(c):
            return _count(key >= c, (0, 1)) + jnp.where(key_new >= c, 1.0, 0.0)

        thr = _kth_largest_key(count_ge, float(ntop), (1, 1))
        above = _count(key > thr, (0, 1)) + jnp.where(key_new > thr, 1.0, 0.0)
        tied = key == thr
        tied_new = key_new == thr
        need = float(ntop) - above

        def count_eq_below(j):
            return _count(tied & (pos < j), (0, 1)) + jnp.where(tied_new & (past < j), 1.0, 0.0)

        lim = _tie_limit(count_eq_below, need, (1, 1), (past + 1).bit_length())
        mask_ref[...] = jnp.where((key > thr) | (tied & (pos < lim)), 1.0, 0.0)
        new_ref[...] = jnp.broadcast_to(
            jnp.where((key_new > thr) | (tied_new & (past < lim)), 1.0, 0.0), new_ref.shape)
        m_ref[...] = jnp.full(m_ref.shape, NEG_INF, F32)
        l_ref[...] = jnp.zeros(l_ref.shape, F32)
        acc_ref[...] = jnp.zeros(acc_ref.shape, F32)

    q = q_ref[...]
    qb = q.astype(BF16)
    kv_head = lax.broadcasted_iota(I32, (heads, 1), 0) // group

    def per_kv_head(f):
        out = f(0)
        for g in range(1, C_KV_HEADS):
            out = jnp.where(kv_head == g, f(g), out)
        return out

    def page_scores(j):
        s_j = per_kv_head(lambda g: _dot_nt(qb, kp_refs[j][:, g, :].astype(BF16))) * scale
        return jnp.where(mask_ref[pl.ds(p * pages_per_step + j, 1), :] > 0.0, s_j, NEG_INF)

    s = jnp.concatenate([page_scores(j) for j in range(pages_per_step)], axis=1)
    m = m_ref[...]
    m_new = jnp.maximum(m, jnp.max(s, axis=-1, keepdims=True))
    a = jnp.exp(m - m_new)
    pr = jnp.where(s > NEG_INF, jnp.exp(s - m_new), 0.0)
    prb = pr.astype(BF16)
    pv = jnp.zeros(acc_ref.shape, F32)
    for j in range(pages_per_step):
        pr_j = prb[:, j * PAGE_SIZE:(j + 1) * PAGE_SIZE]
        pv = pv + per_kv_head(lambda g: _dot(pr_j, vp_refs[j][:, g, :].astype(BF16)))
    l_new = a * l_ref[...] + jnp.sum(pr, axis=-1, keepdims=True)
    acc_new = a * acc_ref[...] + pv
    m_ref[...] = m_new
    l_ref[...] = l_new
    acc_ref[...] = acc_new

    @pl.when(p == n_pages // pages_per_step - 1)
    def _():
        s_new = jnp.sum(q * kn_ref[...], axis=-1, keepdims=True) * scale
        take = new_ref[:, 0:1] > 0.0
        s_new = jnp.where(take, s_new, NEG_INF)
        m_fin = jnp.maximum(m_new, s_new)
        a2 = jnp.exp(m_new - m_fin)
        p_new = jnp.where(take, jnp.exp(s_new - m_fin), 0.0)
        l_fin = a2 * l_new + p_new
        o_ref[...] = (a2 * acc_new + p_new * vn_ref[...]) / l_fin


def _dsa_sample_attn(scores, qi3, wi3, kin3, q3, kn3, vn3, pool_k, pool_v, page_table_flat, heads):
    bsz, n_pages, _ = scores.shape
    ntop = min(DSA_TOPK, (n_pages * PAGE_SIZE + 1) // 4)
    pps = 8 if n_pages % 8 == 0 else 1
    per_b = lambda shape: pl.BlockSpec((None,) + shape, lambda b, p, pt: (b, 0, 0))
    pages = [pl.BlockSpec((None, PAGE_SIZE, C_KV_HEADS, HEAD_DIM),
                          lambda b, p, pt, j=j: (pt[b * n_pages + p * pps + j], 0, 0, 0))
             for j in range(pps)]
    grid_spec = pltpu.PrefetchScalarGridSpec(
        num_scalar_prefetch=1,
        grid=(bsz, n_pages // pps),
        in_specs=[per_b((n_pages, PAGE_SIZE)), per_b((IDX_HEADS, IDX_DIM)), per_b((IDX_HEADS, 1)),
                  per_b((1, IDX_DIM)), per_b((heads, HEAD_DIM)), per_b((heads, HEAD_DIM)),
                  per_b((heads, HEAD_DIM))] + pages + pages,
        out_specs=per_b((heads, HEAD_DIM)),
        scratch_shapes=[pltpu.VMEM((n_pages, PAGE_SIZE), F32), pltpu.VMEM((heads, LANES), F32),
                        pltpu.VMEM((heads, 1), F32), pltpu.VMEM((heads, 1), F32),
                        pltpu.VMEM((heads, HEAD_DIM), F32)],
    )
    body = functools.partial(_dsa_sample_attn_body, n_pages=n_pages, ntop=ntop, heads=heads,
                             pages_per_step=pps)
    return pl.pallas_call(
        body,
        grid_spec=grid_spec,
        out_shape=jax.ShapeDtypeStruct((bsz, heads, HEAD_DIM), F32),
        compiler_params=_cp("parallel", "arbitrary"),
        name="dsa_sample_attn",
    )(page_table_flat, scores, qi3, wi3, kin3, q3, kn3, vn3, *([pool_k] * pps), *([pool_v] * pps))


def _gla_body(q_ref, k_ref, v_ref, r_ref, low_ref, wg2_ref, bg_ref, gn_ref, s0_ref, o_ref, sf_ref,
              state_ref, *, chunk, n_valid, dk, dv):
    c = pl.program_id(2)

    @pl.when(c == 0)
    def _():
        state_ref[...] = s0_ref[...]

    x = _dot_hi(low_ref[...], wg2_ref[...]) + bg_ref[...]
    g = (jnp.minimum(x, 0.0) - jnp.log(1.0 + jnp.exp(-jnp.abs(x)))) / GLA_TAU
    row = lax.broadcasted_iota(I32, (chunk, 1), 0)
    if n_valid < chunk:
        g = jnp.where(row < n_valid, g, 0.0)
    tri = (lax.broadcasted_iota(I32, (chunk, chunk), 1)
           <= lax.broadcasted_iota(I32, (chunk, chunk), 0)).astype(BF16)
    bcum = _dot_exact_lhs(tri, g)
    q = q_ref[...] * (dk ** -0.5)
    k = k_ref[...]
    vb = v_ref[...].astype(BF16)
    sub = 8
    lane = lax.broadcasted_iota(I32, (sub, chunk), 1)
    row_groups = [jnp.zeros((sub, chunk), F32) for _ in range(chunk // sub)]
    for s in range(chunk):
        r0 = (s // sub) * sub
        rows = r0 + lax.broadcasted_iota(I32, (chunk - r0, 1), 0)
        decay = jnp.exp(jnp.where(rows >= s, bcum[r0:, :] - bcum[s:s + 1, :], -jnp.inf))
        col = jnp.sum(q[r0:, :] * k[s:s + 1, :] * decay, axis=-1, keepdims=True)
        for rg in range(s // sub, chunk // sub):
            part = col[rg * sub - r0:(rg + 1) * sub - r0, :]
            row_groups[rg] = jnp.where(lane == s, part, row_groups[rg])
    att = jnp.concatenate(row_groups, axis=0)
    state = state_ref[...]
    o = _dot(att.astype(BF16), vb) + _dot((q * jnp.exp(bcum)).astype(BF16), state.astype(BF16))

    b_last = bcum[chunk - 1:chunk, :]
    last_row = (lax.broadcasted_iota(I32, (chunk, dv), 0) == chunk - 1).astype(BF16)
    b_last_cols = _dot_exact_rhs(bcum, last_row, _dot_tn)
    kd = (k * jnp.exp(b_last - bcum)).astype(BF16)
    new_state = jnp.exp(b_last_cols) * state + _dot_tn(kd, vb)
    state_ref[...] = new_state

    @pl.when(c == pl.num_programs(2) - 1)
    def _():
        sf_ref[...] = new_state

    o_ref[...] = _rms(o) * gn_ref[...] * _silu(r_ref[...])


def _gla(proj3, low3, w_g2, b_g, g_n, s0, *, chunk, n_valid):
    bsz, length, _ = proj3.shape
    heads = GLA_HEADS
    dk, dv = s0.shape[2], s0.shape[3]
    assert dv == 2 * dk
    body = functools.partial(_gla_body, chunk=chunk, n_valid=n_valid, dk=dk, dv=dv)
    return pl.pallas_call(
        body,
        grid=(bsz, heads, length // chunk),
        in_specs=[
            pl.BlockSpec((None, chunk, dk), lambda b, h, c: (b, c, h)),
            pl.BlockSpec((None, chunk, dk), lambda b, h, c: (b, c, heads + h)),
            pl.BlockSpec((None, chunk, dv), lambda b, h, c: (b, c, heads + h)),
            pl.BlockSpec((None, chunk, dv), lambda b, h, c: (b, c, 2 * heads + h)),
            pl.BlockSpec((None, chunk, low3.shape[2]), lambda b, h, c: (b, c, 0)),
            pl.BlockSpec((low3.shape[2], dk), lambda b, h, c: (0, h)),
            pl.BlockSpec((1, dk), lambda b, h, c: (0, h)),
            pl.BlockSpec((1, dv), lambda b, h, c: (0, 0)),
            pl.BlockSpec((None, None, dk, dv), lambda b, h, c: (b, h, 0, 0)),
        ],
        out_specs=[
            pl.BlockSpec((None, chunk, dv), lambda b, h, c: (b, c, h)),
            pl.BlockSpec((None, None, dk, dv), lambda b, h, c: (b, h, 0, 0)),
        ],
        out_shape=[jax.ShapeDtypeStruct((bsz, length, heads * dv), F32),
                   jax.ShapeDtypeStruct(s0.shape, F32)],
        scratch_shapes=[pltpu.VMEM((dk, dv), F32)],
        compiler_params=_cp("parallel", "parallel", "arbitrary"),
        name="gla",
    )(proj3, proj3, proj3, proj3, low3, w_g2, b_g, g_n, s0)


def _mixer_a(xp, xs, batch, seq, norm_mix, layer, cache_k, cache_v, page_table, w_qkv, w_o, g_q, g_k):
    d = xp.shape[1]
    heads = d // HEAD_DIM
    gain = jnp.concatenate([jnp.tile(g_q, heads), jnp.tile(g_k, heads), jnp.ones((d,), F32)])[None, :]

    def qkv(x):
        return _proj(x, w_qkv, norm_gain=norm_mix, norm_layer=layer, head_gain=gain, n_head_norm_cols=2 * d)

    qkv_p = qkv(xp)
    o_p = _moba_prompt_attn(qkv_p, batch, seq, heads)
    xp = _proj(o_p, w_o, residual=xp)

    bsz = xs.shape[0]
    n_pages = page_table.shape[1]
    past = n_pages * PAGE_SIZE
    assert past % MOBA_BLOCK == 0
    nfull = past // MOBA_BLOCK
    topk = min(MOBA_TOPK, nfull)
    assert topk > 0
    qkv_s = qkv(xs)
    qkv_s3 = qkv_s[:, None, :]
    pt_flat = page_table.reshape(-1)
    sel = _moba_sample_select(cache_k, qkv_s[:, :d].reshape(bsz, heads, HEAD_DIM), pt_flat, nfull)
    sel_flat = sel[:, :, :topk].reshape(-1)
    o_s = _moba_sample_attn(qkv_s3, cache_k, cache_v, pt_flat, sel_flat, heads, topk)
    xs = _proj(o_s[:, 0, :], w_o, residual=xs)

    kv_shape = lambda t, n: t.reshape(n, -1, heads, HEAD_DIM)
    outs = (kv_shape(qkv_p[:, d:2 * d], batch), kv_shape(qkv_p[:, 2 * d:], batch),
            kv_shape(qkv_s[:, d:2 * d], bsz), kv_shape(qkv_s[:, 2 * d:], bsz))
    return xp, xs, outs


def _mixer_b(xp, xs, batch, seq, norm_mix, layer, state_pool, w_pool, scale):
    d = xp.shape[1]
    keep = state_pool.shape[1]
    scale = scale[None, :]
    hp = _rmsnorm(xp, norm_mix, layer)
    xp = _pool_mix(hp, xp, w_pool, scale, seq, tm=512)
    pool_p = hp.reshape(batch, seq, d)[:, seq - keep:]

    bsz = xs.shape[0]
    hs = _rmsnorm(xs, norm_mix, layer)
    ext = jnp.concatenate([state_pool, hs[:, None, :]], axis=1)
    assert keep + 1 == POOL_HALO
    x_ext = jnp.concatenate([jnp.zeros_like(state_pool), xs[:, None, :]], axis=1)
    y = _pool_mix(ext.reshape(bsz * POOL_HALO, d), x_ext.reshape(bsz * POOL_HALO, d), w_pool, scale,
                  POOL_HALO, tm=POOL_HALO)
    xs = y.reshape(bsz, POOL_HALO, d)[:, -1]
    return xp, xs, (pool_p, ext[:, 1:])


def _mixer_c(xp, xs, batch, seq, norm_mix, layer, cache_k, cache_v, cache_ki, page_table,
             w_in, w_o, g_q, g_k):
    d = xp.shape[1]
    heads = d // HEAD_DIM
    kvw = C_KV_HEADS * HEAD_DIM
    iw = IDX_HEADS * IDX_DIM
    main = d + 2 * kvw + iw
    w_main, w_tail = w_in[:, :main], w_in[:, main:]
    gain = jnp.concatenate([jnp.tile(g_q, heads), jnp.tile(g_k, C_KV_HEADS),
                            jnp.ones((main - d - kvw,), F32)])[None, :]

    def project(x):
        pm = _proj(x, w_main, norm_gain=norm_mix, norm_layer=layer, head_gain=gain, n_head_norm_cols=d + kvw)
        pt = _proj(x, w_tail, norm_gain=norm_mix, norm_layer=layer)
        return pm, pt

    pm, pt = project(xp)
    k_p, v_p, ki_p = pm[:, d:d + kvw], pm[:, d + kvw:d + 2 * kvw], pt[:, :IDX_DIM]
    o_p = _dsa_prompt_attn(pm, pt, k_p.astype(BF16), v_p.astype(BF16), ki_p.astype(BF16), batch, seq, heads)
    xp = _proj(o_p, w_o, residual=xp)

    bsz = xs.shape[0]
    group = heads // C_KV_HEADS
    sm, st = project(xs)
    k_s, v_s, ki_s = sm[:, d:d + kvw], sm[:, d + kvw:d + 2 * kvw], st[:, :IDX_DIM]
    qi3 = sm[:, d + 2 * kvw:].reshape(bsz, IDX_HEADS, IDX_DIM)
    wi3 = st[:, IDX_DIM:IDX_DIM + IDX_HEADS].reshape(bsz, IDX_HEADS, 1)
    pt_flat = page_table.reshape(-1)
    scores = _dsa_sample_scores(cache_ki, qi3, wi3, pt_flat)
    per_q_head = lambda t: jnp.repeat(t.reshape(bsz, C_KV_HEADS, HEAD_DIM), group, axis=1)
    o_s = _dsa_sample_attn(scores, qi3, wi3, ki_s[:, None, :], sm[:, :d].reshape(bsz, heads, HEAD_DIM),
                           per_q_head(k_s), per_q_head(v_s), cache_k, cache_v, pt_flat, heads)
    xs = _proj(o_s.reshape(bsz, d), w_o, residual=xs)

    kv4 = lambda t, n: t.reshape(n, -1, C_KV_HEADS, HEAD_DIM)
    outs = (kv4(k_p, batch), kv4(v_p, batch), ki_p.reshape(batch, seq, IDX_DIM),
            kv4(k_s, bsz), kv4(v_s, bsz), ki_s.reshape(bsz, 1, IDX_DIM))
    return xp, xs, outs


def _mixer_d(xp, xs, batch, seq, norm_mix, layer, state, w_in, w_g2, b_g, g_n, w_o):
    d = xp.shape[1]
    dk, dv = state.shape[2], state.shape[3]
    main = 2 * GLA_HEADS * dk + 2 * GLA_HEADS * dv
    w_main, w_tail = w_in[:, :main], w_in[:, main:]

    def project(x):
        pm = _proj(x, w_main, norm_gain=norm_mix, norm_layer=layer)
        pt = _proj(x, w_tail, norm_gain=norm_mix, norm_layer=layer)
        return pm, pt

    pm, pt = project(xp)
    zero_state = jnp.zeros((batch,) + state.shape[1:], F32)
    o_p, s_p = _gla(pm.reshape(batch, seq, main), pt.reshape(batch, seq, -1), w_g2, b_g[None, :],
                    g_n[None, :], zero_state, chunk=min(GLA_CHUNK, seq), n_valid=min(GLA_CHUNK, seq))
    xp = _proj(o_p.reshape(batch * seq, GLA_HEADS * dv), w_o, residual=xp)

    bsz = xs.shape[0]
    sm, st = project(xs)
    pad = lambda t: jnp.pad(t[:, None, :], ((0, 0), (0, GLA_SAMPLE_ROWS - 1), (0, 0)))
    o_s, s_s = _gla(pad(sm), pad(st), w_g2, b_g[None, :], g_n[None, :], state,
                    chunk=GLA_SAMPLE_ROWS, n_valid=1)
    xs = _proj(o_s[:, 0, :], w_o, residual=xs)
    return xp, xs, (s_p, s_s)


def kernel(x_prompt, x_sample, cache_a_k, cache_a_v, state_b_pool, cache_c_k, cache_c_v, cache_c_idx_k,
           state_d_gla, page_table, norm_ffn1, ffn1_w_gate, ffn1_w_up, ffn1_w_down, norm_mix, norm_ffn2,
           ffn2_w_gate, ffn2_w_up, ffn2_w_down, a_w_qkv, a_w_o, a_g_q, a_g_k, b_w_pool, b_scale, c_w_in,
           c_w_o, c_g_q, c_g_k, d_w_in, d_w_g2, d_b_g, d_g_n, d_w_o):
    batch, seq, d = x_prompt.shape
    bsz = x_sample.shape[0]
    assert x_sample.shape[1] == 1
    xp = x_prompt.reshape(batch * seq, d)
    xs = x_sample.reshape(bsz, d)
    depth = norm_ffn1.shape[0]
    ffn1_w_gate, ffn1_w_up, ffn1_w_down, ffn2_w_gate, ffn2_w_up, ffn2_w_down = (
        w.astype(BF16) for w in (ffn1_w_gate, ffn1_w_up, ffn1_w_down, ffn2_w_gate, ffn2_w_up, ffn2_w_down))
    outs = {}
    for i in range(depth):
        xp = _ffn(xp, norm_ffn1, ffn1_w_gate, ffn1_w_up, ffn1_w_down, i)
        xs = _ffn(xs, norm_ffn1, ffn1_w_gate, ffn1_w_up, ffn1_w_down, i)
        m = i % 4
        if m == 0:
            xp, xs, outs["a"] = _mixer_a(xp, xs, batch, seq, norm_mix, i, cache_a_k, cache_a_v, page_table,
                                         a_w_qkv, a_w_o, a_g_q, a_g_k)
        elif m == 1:
            xp, xs, outs["b"] = _mixer_b(xp, xs, batch, seq, norm_mix, i, state_b_pool, b_w_pool, b_scale)
        elif m == 2:
            xp, xs, outs["c"] = _mixer_c(xp, xs, batch, seq, norm_mix, i, cache_c_k, cache_c_v, cache_c_idx_k,
                                         page_table, c_w_in, c_w_o, c_g_q, c_g_k)
        else:
            xp, xs, outs["d"] = _mixer_d(xp, xs, batch, seq, norm_mix, i, state_d_gla, d_w_in, d_w_g2,
                                         d_b_g, d_g_n, d_w_o)
        xp = _ffn(xp, norm_ffn2, ffn2_w_gate, ffn2_w_up, ffn2_w_down, i)
        xs = _ffn(xs, norm_ffn2, ffn2_w_gate, ffn2_w_up, ffn2_w_down, i)
    return (xp.reshape(batch, seq, d), xs.reshape(bsz, 1, d)) + outs["a"] + outs["b"] + outs["c"] + outs["d"]
```
